```python
import math
import jax, jax.numpy as jnp
from jax import lax
import numpy as np

D_MODEL = 1024
BATCH = 2
SEQ = 8192
DEPTH = 1
DEC_BATCH = 32
DEC_SEQ = 4
PAST_LEN = 8192
PAGE_SIZE = 128

MIX_WIDTH = D_MODEL
HEAD_DIM = 64
ATT_WIDTH = MIX_WIDTH // 2
N_ATT_HEADS = ATT_WIDTH // HEAD_DIM
DILATED_BRANCHES = ((128, 1), (512, 4), (2048, 16))
MAX_WINDOW = 2048
ROPE_THETA = 10000.0
SSM_WIDTH = MIX_WIDTH - ATT_WIDTH
SSM_HEAD_DIM = 64
N_SSM_HEADS = SSM_WIDTH // SSM_HEAD_DIM
SSM_GROUPS = 2
SSM_STATE = 128
CONV_WIDTH = 4
SSD_CHUNK = 128
CONV_CH = SSM_WIDTH + 2 * SSM_GROUPS * SSM_STATE
IN_SIZES = (ATT_WIDTH, ATT_WIDTH, ATT_WIDTH, SSM_WIDTH, CONV_CH, N_SSM_HEADS)
IN_WIDTH = sum(IN_SIZES)
N_MEM = 256
N_XATT_HEADS = 4
XATT_HEAD_DIM = D_MODEL // N_XATT_HEADS
XATT_WIDTH = N_XATT_HEADS * XATT_HEAD_DIM
D_FF = 4 * D_MODEL
EPS = 1e-6
F32 = jnp.float32

kernel_name = 'dilated_ssd_hybrid_step'


def rmsnorm(x, g):
    xf = x.astype(F32)
    y = xf * lax.rsqrt(jnp.mean(xf * xf, axis=-1, keepdims=True) + EPS)
    return (y * g.astype(F32)).astype(x.dtype)


def rope(t, pos):
    half = HEAD_DIM // 2
    inv = ROPE_THETA ** (-jnp.arange(half, dtype=F32) * 2.0 / HEAD_DIM)
    ang = pos.astype(F32)[:, None] * inv[None, :]
    cos = jnp.cos(ang)[None, :, None, :]
    sin = jnp.sin(ang)[None, :, None, :]
    tf = t.astype(F32)
    t1, t2 = tf[..., :half], tf[..., half:]
    return jnp.concatenate([t1 * cos - t2 * sin, t2 * cos + t1 * sin], axis=-1).astype(t.dtype)


def softmax_stats(s):
    m = jnp.max(s, axis=-1, keepdims=True)
    p = jnp.exp(s - m)
    den = jnp.sum(p, axis=-1, keepdims=True)
    return p / den, (m + jnp.log(den))[..., 0]


def dilated_branch_prompt(q, k, v, window, dil):
    b, s_len, h, dh = q.shape
    n = window // dil
    span = n * dil
    lp = -(-s_len // span) * span
    m_len = lp // dil
    nb = m_len // n

    def to_blocks(t):
        t = jnp.pad(t, ((0, 0), (0, lp - s_len), (0, 0), (0, 0)))
        t = t.reshape(b, m_len, dil, h, dh).transpose(0, 2, 1, 3, 4)
        return t.reshape(b, dil, nb, n, h, dh)

    def with_prev(t):
        prev = jnp.pad(t, ((0, 0), (0, 0), (1, 0), (0, 0), (0, 0), (0, 0)))[:, :, :nb]
        return jnp.concatenate([prev, t], axis=3)

    qb = to_blocks(q)
    kb = with_prev(to_blocks(k))
    vb = with_prev(to_blocks(v))
    s = jnp.einsum('brjqhd,brjkhd->brjhqk', qb, kb) * (HEAD_DIM ** -0.5)
    a = jnp.arange(n)[:, None]
    c = jnp.arange(2 * n)[None, :]
    dist = a + n - c
    band = (dist >= 0) & (dist <= n)
    has_prev = (jnp.arange(nb) > 0)[:, None, None] | (c >= n)[None]
    mask = band[None] & has_prev
    s = jnp.where(mask[None, None, :, None], s, -jnp.inf)
    p, lse = softmax_stats(s)
    o = jnp.einsum('brjhqk,brjkhd->brjqhd', p, vb)
    o = o.reshape(b, dil, m_len, h, dh).transpose(0, 2, 1, 3, 4).reshape(b, lp, h, dh)[:, :s_len]
    lse = lse.transpose(0, 1, 2, 4, 3).reshape(b, dil, m_len, h)
    lse = lse.transpose(0, 2, 1, 3).reshape(b, lp, h)[:, :s_len]
    return o, lse


def dilated_branch_sample(q, k_all, v_all, window, dil):
    t_len = q.shape[1]
    lb = k_all.shape[1] - t_len
    n = window // dil
    idx = lb + jnp.arange(t_len)[:, None] - dil * jnp.arange(n + 1)[None, :]
    valid = idx >= 0
    idx = jnp.maximum(idx, 0)
    kg = k_all[:, idx]
    vg = v_all[:, idx]
    s = jnp.einsum('bthd,btkhd->bthk', q, kg) * (HEAD_DIM ** -0.5)
    s = jnp.where(valid[None, :, None, :], s, -jnp.inf)
    p, lse = softmax_stats(s)
    return jnp.einsum('bthk,btkhd->bthd', p, vg), lse


def combine_branches(outs, lses):
    w = jax.nn.softmax(jnp.stack(lses, axis=0), axis=0)
    return jnp.sum(w[..., None] * jnp.stack(outs, axis=0), axis=0)


def ssd_scan(xs, dt, a_neg, bm, cm, h0):
    b, l_len, nh, hp = xs.shape
    q = min(SSD_CHUNK, l_len)
    lp = -(-l_len // q) * q
    nc = lp // q

    def pad(t):
        return jnp.pad(t, ((0, 0), (0, lp - l_len)) + ((0, 0),) * (t.ndim - 2))

    rep = nh // SSM_GROUPS
    bh = jnp.repeat(pad(bm), rep, axis=2).reshape(b, nc, q, nh, SSM_STATE)
    ch = jnp.repeat(pad(cm), rep, axis=2).reshape(b, nc, q, nh, SSM_STATE)
    dtp = pad(dt)
    xdt = (pad(xs) * dtp[..., None]).reshape(b, nc, q, nh, hp)
    a = (dtp * a_neg).reshape(b, nc, q, nh).transpose(0, 3, 1, 2)
    a_cs = jnp.cumsum(a, axis=-1)
    tri = jnp.tril(jnp.ones((q, q), dtype=bool))
    decay_in = jnp.exp(jnp.where(tri, a_cs[..., :, None] - a_cs[..., None, :], -jnp.inf))
    scores = jnp.einsum('bclhn,bcshn->bhcls', ch, bh) * decay_in
    y_diag = jnp.einsum('bhcls,bcshp->bclhp', scores, xdt)
    to_end = jnp.exp(a_cs[..., -1:] - a_cs).transpose(0, 2, 3, 1)
    chunk_states = jnp.einsum('bclhn,bclhp->bchpn', bh, xdt * to_end[..., None])
    chunk_decay = jnp.exp(a_cs[..., -1])

    def step(h, inp):
        st, dec = inp
        return h * dec[..., None, None] + st, h

    h_last, h_start = lax.scan(step, h0, (chunk_states.transpose(1, 0, 2, 3, 4),
                                          chunk_decay.transpose(2, 0, 1)))
    h_start = h_start.transpose(1, 0, 2, 3, 4)
    from_start = jnp.exp(a_cs).transpose(0, 2, 3, 1)
    y_off = jnp.einsum('bclhn,bchpn->bclhp', ch, h_start) * from_start[..., None]
    y = (y_diag + y_off).reshape(b, lp, nh, hp)[:, :l_len]
    return y, h_last


def gated_rmsnorm(y, z, g):
    b, l_len, w = y.shape
    u = (y * jax.nn.silu(z.astype(F32))).reshape(b, l_len, SSM_GROUPS, w // SSM_GROUPS)
    u = u * lax.rsqrt(jnp.mean(u * u, axis=-1, keepdims=True) + EPS)
    return u.reshape(b, l_len, w) * g.astype(F32)


def mixer(h, pos, k_past, v_past, conv_prev, ssm_prev,
          w_in, conv_w, conv_b, dt_bias, a_log, d_skip, g_ssm, w_out):
    b, l_len, _ = h.shape
    offs = np.cumsum(IN_SIZES)[:-1].tolist()
    q, k, v, z, xbc, dt = jnp.split(h @ w_in, offs, axis=-1)
    q = rope(q.reshape(b, l_len, N_ATT_HEADS, HEAD_DIM), pos).astype(F32)
    k = rope(k.reshape(b, l_len, N_ATT_HEADS, HEAD_DIM), pos)
    v = v.reshape(b, l_len, N_ATT_HEADS, HEAD_DIM)
    outs, lses = [], []
    if k_past is None:
        for window, dil in DILATED_BRANCHES:
            o, l = dilated_branch_prompt(q, k.astype(F32), v.astype(F32), window, dil)
            outs.append(o)
            lses.append(l)
        lw = min(MAX_WINDOW, l_len)
        new_k, new_v = k[:, l_len - lw:], v[:, l_len - lw:]
        conv_prev = jnp.zeros((b, CONV_WIDTH - 1, CONV_CH), xbc.dtype)
        ssm_prev = jnp.zeros((b, N_SSM_HEADS, SSM_HEAD_DIM, SSM_STATE), F32)
    else:
        lb = k_past.shape[1]
        k_all = jnp.concatenate([k_past.astype(k.dtype), k], axis=1)
        v_all = jnp.concatenate([v_past.astype(v.dtype), v], axis=1)
        for window, dil in DILATED_BRANCHES:
            o, l = dilated_branch_sample(q, k_all.astype(F32), v_all.astype(F32), window, dil)
            outs.append(o)
            lses.append(l)
        new_k, new_v = k_all[:, -lb:], v_all[:, -lb:]
    att = combine_branches(outs, lses)

    xp = jnp.concatenate([conv_prev.astype(xbc.dtype), xbc], axis=1)
    new_conv = xp[:, -(CONV_WIDTH - 1):]
    xf = xp.astype(F32)
    conv = conv_b.astype(F32)
    for i in range(CONV_WIDTH):
        conv = conv + xf[:, i:i + l_len] * conv_w[i].astype(F32)
    conv = jax.nn.silu(conv)
    xs, bm, cm = jnp.split(conv, [SSM_WIDTH, SSM_WIDTH + SSM_GROUPS * SSM_STATE], axis=-1)
    xs = xs.reshape(b, l_len, N_SSM_HEADS, SSM_HEAD_DIM)
    bm = bm.reshape(b, l_len, SSM_GROUPS, SSM_STATE)
    cm = cm.reshape(b, l_len, SSM_GROUPS, SSM_STATE)
    dtv = jax.nn.softplus(dt.astype(F32) + dt_bias.astype(F32))
    a_neg = -jnp.exp(a_log.astype(F32))
    y, new_ssm = ssd_scan(xs, dtv, a_neg, bm, cm, ssm_prev.astype(F32))
    y = (y + d_skip.astype(F32)[:, None] * xs).reshape(b, l_len, SSM_WIDTH)
    y = gated_rmsnorm(y, z, g_ssm)
    mixed = jnp.concatenate([att.reshape(b, l_len, ATT_WIDTH), y], axis=-1).astype(h.dtype)
    return mixed @ w_out, (new_k, new_v, new_conv, new_ssm)


def memory_kv(mem, g_mem, w_mk, w_mv):
    b = mem.shape[0]
    m = rmsnorm(mem, g_mem)
    mk = (m @ w_mk).reshape(b, -1, N_XATT_HEADS, XATT_HEAD_DIM)
    mv = (m @ w_mv).reshape(b, -1, N_XATT_HEADS, XATT_HEAD_DIM)
    return mk, mv


def cross_attention(h, mem_k, mem_v, w_xq, w_xo):
    b, l_len, _ = h.shape
    q = (h @ w_xq).reshape(b, l_len, N_XATT_HEADS, XATT_HEAD_DIM).astype(F32)
    s = jnp.einsum('blhd,bmhd->bhlm', q, mem_k.astype(F32)) * (XATT_HEAD_DIM ** -0.5)
    p = jax.nn.softmax(s, axis=-1)
    o = jnp.einsum('bhlm,bmhd->blhd', p, mem_v.astype(F32)).reshape(b, l_len, XATT_WIDTH)
    return o.astype(h.dtype) @ w_xo


def trunk_layer(x, pos, k_past, v_past, conv_prev, ssm_prev, mem_k, mem_v,
                g_mix, w_in, conv_w, conv_b, dt_bias, a_log, d_skip, g_ssm, w_out,
                g_xatt, w_xq, w_xo, g_mlp, w_up, w_down):
    mix, new_state = mixer(rmsnorm(x, g_mix), pos, k_past, v_past, conv_prev, ssm_prev,
                           w_in, conv_w, conv_b, dt_bias, a_log, d_skip, g_ssm, w_out)
    x = x + mix
    x = x + cross_attention(rmsnorm(x, g_xatt), mem_k, mem_v, w_xq, w_xo)
    x = x + jnp.square(jax.nn.relu(rmsnorm(x, g_mlp) @ w_up)) @ w_down
    return x, new_state


def setup_inputs(seed: int = 0) -> dict:
    key = jax.random.key(seed)
    ks = list(jax.random.split(key, 32))

    def nrm(i, shape, scale):
        return jax.random.normal(ks[i], shape, F32) * scale

    lw = min(MAX_WINDOW, PAST_LEN)
    u = jax.random.uniform(ks[20], (DEPTH, N_SSM_HEADS), F32)
    dt0 = jnp.exp(u * (math.log(0.1) - math.log(1e-3)) + math.log(1e-3))
    return {
        'x_prompt': nrm(0, (BATCH, SEQ, D_MODEL), 1.0),
        'x_sample': nrm(1, (DEC_BATCH, DEC_SEQ, D_MODEL), 1.0),
        'cache_win_k': nrm(2, (DEPTH, DEC_BATCH, lw, N_ATT_HEADS, HEAD_DIM), 1.0),
        'cache_win_v': nrm(3, (DEPTH, DEC_BATCH, lw, N_ATT_HEADS, HEAD_DIM), 1.0),
        'state_conv': nrm(4, (DEPTH, DEC_BATCH, CONV_WIDTH - 1, CONV_CH), 1.0),
        'state_ssm': nrm(5, (DEPTH, DEC_BATCH, N_SSM_HEADS, SSM_HEAD_DIM, SSM_STATE), 0.5),
        'cache_mem_k': nrm(6, (DEPTH, DEC_BATCH, N_MEM, N_XATT_HEADS, XATT_HEAD_DIM), 1.0),
        'cache_mem_v': nrm(7, (DEPTH, DEC_BATCH, N_MEM, N_XATT_HEADS, XATT_HEAD_DIM), 1.0),
        'mem_prompt': nrm(8, (BATCH, N_MEM, D_MODEL), 1.0),
        'g_mix': 1.0 + nrm(9, (DEPTH, D_MODEL), 0.01),
        'w_in': nrm(10, (DEPTH, D_MODEL, IN_WIDTH), D_MODEL ** -0.5),
        'conv_w': nrm(11, (DEPTH, CONV_WIDTH, CONV_CH), CONV_WIDTH ** -0.5),
        'conv_b': nrm(12, (DEPTH, CONV_CH), 0.01),
        'dt_bias': dt0 + jnp.log(-jnp.expm1(-dt0)),
        'a_log': jnp.log(jax.random.uniform(ks[21], (DEPTH, N_SSM_HEADS), F32, 1.0, 16.0)),
        'd_skip': 1.0 + nrm(13, (DEPTH, N_SSM_HEADS), 0.01),
        'g_ssm': 1.0 + nrm(14, (DEPTH, SSM_WIDTH), 0.01),
        'w_out': nrm(15, (DEPTH, MIX_WIDTH, D_MODEL), MIX_WIDTH ** -0.5),
        'g_xatt': 1.0 + nrm(16, (DEPTH, D_MODEL), 0.01),
        'g_mem': 1.0 + nrm(17, (DEPTH, D_MODEL), 0.01),
        'w_xq': nrm(18, (DEPTH, D_MODEL, XATT_WIDTH), D_MODEL ** -0.5),
        'w_mk': nrm(19, (DEPTH, D_MODEL, XATT_WIDTH), D_MODEL ** -0.5),
        'w_mv': nrm(22, (DEPTH, D_MODEL, XATT_WIDTH), D_MODEL ** -0.5),
        'w_xo': nrm(23, (DEPTH, XATT_WIDTH, D_MODEL), XATT_WIDTH ** -0.5),
        'g_mlp': 1.0 + nrm(24, (DEPTH, D_MODEL), 0.01),
        'w_up': nrm(25, (DEPTH, D_MODEL, D_FF), D_MODEL ** -0.5),
        'w_down': nrm(26, (DEPTH, D_FF, D_MODEL), D_FF ** -0.5),
        'g_final': 1.0 + nrm(27, (D_MODEL,), 0.01),
    }


def reference(x_prompt, x_sample, cache_win_k, cache_win_v, state_conv, state_ssm,
              cache_mem_k, cache_mem_v, mem_prompt,
              g_mix, w_in, conv_w, conv_b, dt_bias, a_log, d_skip, g_ssm, w_out,
              g_xatt, g_mem, w_xq, w_mk, w_mv, w_xo, g_mlp, w_up, w_down, g_final):
    pos_p = jnp.arange(x_prompt.shape[1], dtype=jnp.int32)
    pos_s = PAST_LEN + jnp.arange(x_sample.shape[1], dtype=jnp.int32)
    hp, hs = x_prompt, x_sample
    wk_p, wv_p, cv_p, ss_p, mk_p, mv_p = [], [], [], [], [], []
    wk_s, wv_s, cv_s, ss_s = [], [], [], []
    for l in range(DEPTH):
        lw = (g_mix[l], w_in[l], conv_w[l], conv_b[l], dt_bias[l], a_log[l], d_skip[l],
              g_ssm[l], w_out[l], g_xatt[l], w_xq[l], w_xo[l], g_mlp[l], w_up[l], w_down[l])
        mk, mv = memory_kv(mem_prompt, g_mem[l], w_mk[l], w_mv[l])
        hp, (nk, nv, nc, ns) = trunk_layer(hp, pos_p, None, None, None, None, mk, mv, *lw)
        wk_p.append(nk)
        wv_p.append(nv)
        cv_p.append(nc)
        ss_p.append(ns)
        mk_p.append(mk)
        mv_p.append(mv)
        hs, (nk, nv, nc, ns) = trunk_layer(hs, pos_s, cache_win_k[l], cache_win_v[l],
                                           state_conv[l], state_ssm[l],
                                           cache_mem_k[l], cache_mem_v[l], *lw)
        wk_s.append(nk)
        wv_s.append(nv)
        cv_s.append(nc)
        ss_s.append(ns)
    y_prompt = rmsnorm(hp, g_final)
    y_sample = rmsnorm(hs, g_final)
    return (y_prompt, y_sample,
            jnp.stack(wk_p), jnp.stack(wv_p), jnp.stack(cv_p), jnp.stack(ss_p),
            jnp.stack(mk_p), jnp.stack(mv_p),
            jnp.stack(wk_s), jnp.stack(wv_s), jnp.stack(cv_s), jnp.stack(ss_s))
```

```python
import functools

import jax
import jax.numpy as jnp
from jax import lax
from jax.experimental import pallas as pl
from jax.experimental.pallas import tpu as pltpu

F32 = jnp.float32
BF16 = jnp.bfloat16

D_MODEL = 1024
PAST_LEN = 8192
HEAD_DIM = 64
ATT_WIDTH = 512
N_ATT_HEADS = 8
DILATED_BRANCHES = ((128, 1), (512, 4), (2048, 16))
MAX_WINDOW = 2048
ROPE_THETA = 10000.0
SSM_WIDTH = 512
N_SSM_HEADS = 8
SSM_GROUPS = 2
SSM_STATE = 128
CONV_WIDTH = 4
SSD_CHUNK = 128
CONV_CH = 1024
N_MEM = 256
N_XATT_HEADS = 4
XATT_HEAD_DIM = 256
D_FF = 4096
EPS = 1e-6

LANES = 128
PAIR = 2 * HEAD_DIM
N_PAIRS = ATT_WIDTH // PAIR
BRANCH_BLOCK = 128
SAMPLE_ROWS = 16
VMEM_LIMIT = 56 * 1024 * 1024


def _params(*sem):
    return pltpu.CompilerParams(dimension_semantics=sem, vmem_limit_bytes=VMEM_LIMIT)


def _const_spec(shape):
    return pl.BlockSpec(shape, lambda *_: (0,) * len(shape), pipeline_mode=pl.Buffered(1))


def _rmsnorm(x, g):
    return x * lax.rsqrt(jnp.mean(x * x, axis=-1, keepdims=True) + EPS) * g


def _dot(a, b):
    return jnp.dot(a, b, preferred_element_type=F32)


def _dot_nt(a, b):
    return lax.dot_general(a, b, (((1,), (1,)), ((), ())), preferred_element_type=F32)


def _dot_tn(a, b):
    return lax.dot_general(a, b, (((0,), (0,)), ((), ())), preferred_element_type=F32)


def _in_proj_body(x_ref, g_ref, w_ref, wdt_ref, cos_ref, sin_ref,
                  q_ref, k_ref, v_ref, k32_ref, v32_ref, z_ref, xbc_ref, dt_ref):
    h = _rmsnorm(x_ref[0], g_ref[...]).astype(BF16)
    tm = h.shape[0]
    cos = jnp.concatenate([cos_ref[...]] * N_PAIRS, axis=1)
    sin = jnp.concatenate([sin_ref[...]] * N_PAIRS, axis=1)
    lane = lax.broadcasted_iota(jnp.int32, (tm, ATT_WIDTH), 1)
    first_half = (lane & (HEAD_DIM // 2)) == 0

    def rope(t):
        partner = jnp.where(first_half,
                            pltpu.roll(t, ATT_WIDTH - HEAD_DIM // 2, axis=1),
                            pltpu.roll(t, HEAD_DIM // 2, axis=1))
        return t * cos + partner * sin

    q = rope(_dot(h, w_ref[:, 0:512]))
    k = rope(_dot(h, w_ref[:, 512:1024]))
    v = _dot(h, w_ref[:, 1024:1536])
    q_ref[0] = q.astype(BF16)
    k_ref[0] = k.astype(BF16)
    v_ref[0] = v.astype(BF16)
    k32_ref[0] = k
    v32_ref[0] = v
    z_ref[0] = _dot(h, w_ref[:, 1536:2048])
    xbc_ref[0] = _dot(h, w_ref[:, 2048:3072])
    dt_ref[0] = _dot(h, wdt_ref[...])


def _in_proj(x, g, w_main, w_dt, cos, sin, window, tm):
    b, l, _ = x.shape
    nj = l // tm
    j0 = (l - window) // tm
    row = lambda bi, j: (bi, j, 0)
    win = lambda bi, j: (bi, jnp.maximum(j - j0, 0), 0)
    outs = (
        jax.ShapeDtypeStruct((b, l, ATT_WIDTH), BF16),
        jax.ShapeDtypeStruct((b, l, ATT_WIDTH), BF16),
        jax.ShapeDtypeStruct((b, l, ATT_WIDTH), BF16),
        jax.ShapeDtypeStruct((b, window, ATT_WIDTH), F32),
        jax.ShapeDtypeStruct((b, window, ATT_WIDTH), F32),
        jax.ShapeDtypeStruct((b, l, SSM_WIDTH), F32),
        jax.ShapeDtypeStruct((b, l, CONV_CH), F32),
        jax.ShapeDtypeStruct((b, l, LANES), F32),
    )
    return pl.pallas_call(
        _in_proj_body,
        grid=(b, nj),
        in_specs=[
            pl.BlockSpec((1, tm, D_MODEL), row),
            _const_spec((1, D_MODEL)),
            _const_spec(w_main.shape),
            _const_spec(w_dt.shape),
            pl.BlockSpec((tm, LANES), lambda bi, j: (j, 0)),
            pl.BlockSpec((tm, LANES), lambda bi, j: (j, 0)),
        ],
        out_specs=(
            pl.BlockSpec((1, tm, ATT_WIDTH), row),
            pl.BlockSpec((1, tm, ATT_WIDTH), row),
            pl.BlockSpec((1, tm, ATT_WIDTH), row),
            pl.BlockSpec((1, tm, ATT_WIDTH), win),
            pl.BlockSpec((1, tm, ATT_WIDTH), win),
            pl.BlockSpec((1, tm, SSM_WIDTH), row),
            pl.BlockSpec((1, tm, CONV_CH), row),
            pl.BlockSpec((1, tm, LANES), row),
        ),
        out_shape=outs,
        compiler_params=_params("arbitrary", "arbitrary"),
        name="in_proj",
    )(x, g, w_main, w_dt, cos, sin)


def _attn_branch_body(q_ref, kc_ref, kp_ref, vc_ref, vp_ref, o_ref, lse_ref, kall_ref, vall_ref, *, nblk):
    j = pl.program_id(2)
    n = BRANCH_BLOCK
    kall_ref[0:n, :] = kp_ref[0]
    kall_ref[n:, :] = kc_ref[0]
    vall_ref[0:n, :] = vp_ref[0]
    vall_ref[n:, :] = vc_ref[0]
    row = lax.broadcasted_iota(jnp.int32, (n, n), 0)
    col = lax.broadcasted_iota(jnp.int32, (n, n), 1)
    in_prev = col >= row
    in_cur = col <= row
    low = lax.broadcasted_iota(jnp.int32, (n, PAIR), 1) < HEAD_DIM
    scale = HEAD_DIM ** -0.5

    def block(g, carry):
        r0 = pl.multiple_of(g * n, n)
        r1 = pl.multiple_of(g * n + n, n)
        has_prev = (j * nblk + g) > 0
        prev_ok = jnp.logical_and(in_prev, has_prev)
        for pr in range(N_PAIRS):
            sl = slice(pr * PAIR, (pr + 1) * PAIR)
            qp = q_ref[0, pl.ds(r0, n), sl].astype(F32)
            k_prev = kall_ref[pl.ds(r0, n), sl]
            k_cur = kall_ref[pl.ds(r1, n), sl]
            v_prev = vall_ref[pl.ds(r0, n), sl]
            v_cur = vall_ref[pl.ds(r1, n), sl]
            o_h, l_h = [], []
            for hh in range(2):
                qm = jnp.where(low if hh == 0 else jnp.logical_not(low), qp, 0.0).astype(BF16)
                s_p = jnp.where(prev_ok, _dot_nt(qm, k_prev) * scale, -jnp.inf)
                s_c = jnp.where(in_cur, _dot_nt(qm, k_cur) * scale, -jnp.inf)
                m = jnp.maximum(jnp.max(s_p, axis=-1, keepdims=True), jnp.max(s_c, axis=-1, keepdims=True))
                p_p = jnp.exp(s_p - m)
                p_c = jnp.exp(s_c - m)
                den = jnp.sum(p_p, axis=-1, keepdims=True) + jnp.sum(p_c, axis=-1, keepdims=True)
                pv = _dot(p_p.astype(BF16), v_prev) + _dot(p_c.astype(BF16), v_cur)
                o_h.append(pv / den)
                l_h.append(jnp.broadcast_to(m + jnp.log(den), (n, PAIR)))
            o_ref[0, pl.ds(r0, n), sl] = jnp.where(low, o_h[0], o_h[1])
            lse_ref[0, pl.ds(r0, n), sl] = jnp.where(low, l_h[0], l_h[1])
        return carry

    lax.fori_loop(0, nblk, block, 0)


def _attn_branch(q, k, v, dil):
    b, s, w = q.shape
    n = BRANCH_BLOCK
    m_len = s // dil
    nblk = min(4, m_len // n)
    rows = nblk * n
    view = lambda t: t.reshape(b, m_len, dil * w)
    cur = lambda bi, r, j: (bi, j, r)
    prev = lambda bi, r, j: (bi, jnp.maximum(j * nblk - 1, 0), r)
    o, lse = pl.pallas_call(
        functools.partial(_attn_branch_body, nblk=nblk),
        grid=(b, dil, m_len // rows),
        in_specs=[
            pl.BlockSpec((1, rows, w), cur),
            pl.BlockSpec((1, rows, w), cur),
            pl.BlockSpec((1, n, w), prev),
            pl.BlockSpec((1, rows, w), cur),
            pl.BlockSpec((1, n, w), prev),
        ],
        out_specs=(pl.BlockSpec((1, rows, w), cur), pl.BlockSpec((1, rows, w), cur)),
        out_shape=(jax.ShapeDtypeStruct((b, m_len, dil * w), F32),
                   jax.ShapeDtypeStruct((b, m_len, dil * w), F32)),
        scratch_shapes=[pltpu.VMEM((rows + n, w), BF16), pltpu.VMEM((rows + n, w), BF16)],
        compiler_params=_params("arbitrary", "arbitrary", "arbitrary"),
        name=f"attn_branch_d{dil}",
    )(view(q), view(k), view(k), view(v), view(v))
    return o.reshape(b, s, w), lse.reshape(b, s, w)


def _branch_count(dist):
    cnt = jnp.zeros(dist.shape, F32)
    for window, dil in DILATED_BRANCHES:
        hit = (dist >= 0) & (dist <= window) & ((dist & (dil - 1)) == 0)
        cnt = cnt + hit.astype(F32)
    return cnt


def _attn_sample_body(q_ref, kn_ref, vn_ref, kc_ref, vc_ref, att_ref, ko_ref, vo_ref, *, n_new):
    lb = kc_ref.shape[1]
    rows = q_ref.shape[1]
    k_cache = kc_ref[0]
    v_cache = vc_ref[0]
    k_new = kn_ref[0]
    v_new = vn_ref[0]
    ko_ref[0, 0:lb - n_new, :] = kc_ref[0, n_new:lb, :]
    ko_ref[0, lb - n_new:lb, :] = k_new[0:n_new]
    vo_ref[0, 0:lb - n_new, :] = vc_ref[0, n_new:lb, :]
    vo_ref[0, lb - n_new:lb, :] = v_new[0:n_new]

    t_c = lax.broadcasted_iota(jnp.int32, (rows, lb), 0)
    i_c = lax.broadcasted_iota(jnp.int32, (rows, lb), 1)
    cnt_c = _branch_count(lb + t_c - i_c)
    t_n = lax.broadcasted_iota(jnp.int32, (rows, rows), 0)
    i_n = lax.broadcasted_iota(jnp.int32, (rows, rows), 1)
    cnt_n = jnp.where(i_n < n_new, _branch_count(t_n - i_n), 0.0)
    low = lax.broadcasted_iota(jnp.int32, (rows, PAIR), 1) < HEAD_DIM
    scale = HEAD_DIM ** -0.5
    kcb = k_cache.astype(BF16)
    vcb = v_cache.astype(BF16)
    knb = k_new.astype(BF16)
    vnb = v_new.astype(BF16)
    for pr in range(N_PAIRS):
        sl = slice(pr * PAIR, (pr + 1) * PAIR)
        qp = q_ref[0, :, sl].astype(F32)
        o_h = []
        for hh in range(2):
            qm = jnp.where(low if hh == 0 else jnp.logical_not(low), qp, 0.0).astype(BF16)
            s_c = jnp.where(cnt_c > 0, _dot_nt(qm, kcb[:, sl]) * scale, -jnp.inf)
            s_n = jnp.where(cnt_n > 0, _dot_nt(qm, knb[:, sl]) * scale, -jnp.inf)
            m = jnp.maximum(jnp.max(s_c, axis=-1, keepdims=True), jnp.max(s_n, axis=-1, keepdims=True))
            p_c = cnt_c * jnp.exp(s_c - m)
            p_n = cnt_n * jnp.exp(s_n - m)
            den = jnp.sum(p_c, axis=-1, keepdims=True) + jnp.sum(p_n, axis=-1, keepdims=True)
            pv = _dot(p_c.astype(BF16), vcb[:, sl]) + _dot(p_n.astype(BF16), vnb[:, sl])
            o_h.append(pv / den)
        att_ref[0, :, sl] = jnp.where(low, o_h[0], o_h[1])


def _attn_sample(q, k_new, v_new, k_cache, v_cache, n_new):
    b, rows, w = q.shape
    lb = k_cache.shape[1]
    small = lambda bi: (bi, 0, 0)
    return pl.pallas_call(
        functools.partial(_attn_sample_body, n_new=n_new),
        grid=(b,),
        in_specs=[
            pl.BlockSpec((1, rows, w), small),
            pl.BlockSpec((1, rows, w), small),
            pl.BlockSpec((1, rows, w), small),
            pl.BlockSpec((1, lb, w), small),
            pl.BlockSpec((1, lb, w), small),
        ],
        out_specs=(pl.BlockSpec((1, rows, w), small),
                   pl.BlockSpec((1, lb, w), small),
                   pl.BlockSpec((1, lb, w), small)),
        out_shape=(jax.ShapeDtypeStruct((b, rows, w), F32),
                   jax.ShapeDtypeStruct((b, lb, w), F32),
                   jax.ShapeDtypeStruct((b, lb, w), F32)),
        compiler_params=_params("arbitrary"),
        name="attn_sample",
    )(q, k_new, v_new, k_cache, v_cache)


def _ssd_body(xbc_ref, z_ref, dt_ref, cprev_ref, h0_ref, cw_ref, cb_ref, dtb_ref, alog_ref, dskip_ref, g_ref,
              y_ref, hout_ref, h_ref, xprev_ref, *, rows_in, n_valid):
    c = pl.program_id(1)
    q = SSD_CHUNK

    @pl.when(c == 0)
    def _():
        h_ref[...] = h0_ref[0]
        xprev_ref[...] = cprev_ref[0]

    def chunk_rows(ref):
        t = ref[0]
        if rows_in < q:
            t = jnp.concatenate([t, jnp.zeros((q - rows_in, t.shape[1]), t.dtype)], axis=0)
        return t

    cur = chunk_rows(xbc_ref)
    prev8 = xprev_ref[...]
    i8 = lax.broadcasted_iota(jnp.int32, (8, CONV_CH), 0)
    acc = cb_ref[...] + cur * cw_ref[CONV_WIDTH - 1:CONV_WIDTH, :]
    for sh in range(1, CONV_WIDTH):
        rolled = pltpu.roll(cur, sh, axis=0)
        head = jnp.where(i8 < sh, pltpu.roll(prev8, sh, axis=0), rolled[0:8])
        shifted = jnp.concatenate([head, rolled[8:]], axis=0)
        acc = acc + shifted * cw_ref[CONV_WIDTH - 1 - sh:CONV_WIDTH - sh, :]
    xprev_ref[...] = cur[q - 8:q]
    conv = acc * (1.0 / (1.0 + jnp.exp(-acc)))
    xs = conv[:, 0:SSM_WIDTH]

    row_l = lax.broadcasted_iota(jnp.int32, (q, LANES), 0)
    lane_l = lax.broadcasted_iota(jnp.int32, (q, LANES), 1)
    dt_in = chunk_rows(dt_ref) + dtb_ref[...]
    dtv = jnp.maximum(dt_in, 0.0) + jnp.log1p(jnp.exp(-jnp.abs(dt_in)))
    dtv = jnp.where((c * q + row_l < n_valid) & (lane_l < N_SSM_HEADS), dtv, 0.0)
    a = dtv * (-jnp.exp(alog_ref[...]))
    tri = row_l >= lane_l
    a_cs = jnp.dot(tri.astype(F32), a, preferred_element_type=F32, precision=lax.Precision.HIGHEST)
    a_cs_t = a_cs.T
    low = lax.broadcasted_iota(jnp.int32, (q, PAIR), 1) < HEAD_DIM

    def per_pair(col0, col1):
        return jnp.where(low, col0, col1)

    ys = []
    for pr in range(N_SSM_HEADS // 2):
        grp = (2 * pr) // (N_SSM_HEADS // SSM_GROUPS)
        sl = slice(pr * PAIR, (pr + 1) * PAIR)
        bm = conv[:, SSM_WIDTH + grp * SSM_STATE:SSM_WIDTH + (grp + 1) * SSM_STATE]
        cm = conv[:, SSM_WIDTH + (SSM_GROUPS + grp) * SSM_STATE:SSM_WIDTH + (SSM_GROUPS + grp + 1) * SSM_STATE]
        bmb = bm.astype(BF16)
        cmb = cm.astype(BF16)
        cb = _dot_nt(cmb, bmb)
        h0, h1 = 2 * pr, 2 * pr + 1
        cs0, cs1 = a_cs[:, h0:h0 + 1], a_cs[:, h1:h1 + 1]
        tot0, tot1 = a_cs[q - 1:q, h0:h0 + 1], a_cs[q - 1:q, h1:h1 + 1]
        xs_p = xs[:, sl]
        xdt = xs_p * per_pair(dtv[:, h0:h0 + 1], dtv[:, h1:h1 + 1])
        xdt_b = xdt.astype(BF16)
        y_d = []
        for hh, cs in ((h0, cs0), (h1, cs1)):
            decay = jnp.exp(jnp.where(tri, cs - a_cs_t[hh:hh + 1, :], -jnp.inf))
            y_d.append(_dot((cb * decay).astype(BF16), xdt_b))
        y_diag = per_pair(y_d[0], y_d[1])
        to_end = per_pair(jnp.exp(tot0 - cs0), jnp.exp(tot1 - cs1))
        states = _dot_tn((xdt * to_end).astype(BF16), bmb)
        h_pair = h_ref[pr * PAIR:(pr + 1) * PAIR, :]
        y_off = _dot_nt(cmb, h_pair.astype(BF16)) * per_pair(jnp.exp(cs0), jnp.exp(cs1))
        chunk_decay = jnp.concatenate([jnp.broadcast_to(jnp.exp(tot0), (HEAD_DIM, SSM_STATE)),
                                       jnp.broadcast_to(jnp.exp(tot1), (HEAD_DIM, SSM_STATE))], axis=0)
        h_ref[pr * PAIR:(pr + 1) * PAIR, :] = h_pair * chunk_decay + states
        ys.append(y_diag + y_off + dskip_ref[:, sl] * xs_p)
    y = jnp.concatenate(ys, axis=1)

    zz = chunk_rows(z_ref)
    u = y * (zz * (1.0 / (1.0 + jnp.exp(-zz))))
    gw = SSM_WIDTH // SSM_GROUPS
    normed = []
    for grp in range(SSM_GROUPS):
        ug = u[:, grp * gw:(grp + 1) * gw]
        normed.append(ug * lax.rsqrt(jnp.mean(ug * ug, axis=-1, keepdims=True) + EPS))
    out = jnp.concatenate(normed, axis=1) * g_ref[...]
    y_ref[0] = out[0:rows_in]
    hout_ref[0] = h_ref[...]


def _ssd(xbc, z, dt, conv_prev8, h0, conv_w8, conv_b, dt_bias, a_log, d_skip, g_ssm, n_valid):
    b, l, _ = xbc.shape
    rows_in = min(SSD_CHUNK, l)
    nc = l // rows_in
    row = lambda bi, c: (bi, c, 0)
    per_b = lambda bi, c: (bi, 0, 0)
    return pl.pallas_call(
        functools.partial(_ssd_body, rows_in=rows_in, n_valid=n_valid),
        grid=(b, nc),
        in_specs=[
            pl.BlockSpec((1, rows_in, CONV_CH), row),
            pl.BlockSpec((1, rows_in, SSM_WIDTH), row),
            pl.BlockSpec((1, rows_in, LANES), row),
            pl.BlockSpec((1, 8, CONV_CH), per_b),
            pl.BlockSpec((1, SSM_WIDTH, SSM_STATE), per_b),
            _const_spec((8, CONV_CH)),
            _const_spec((1, CONV_CH)),
            _const_spec((1, LANES)),
            _const_spec((1, LANES)),
            _const_spec((1, SSM_WIDTH)),
            _const_spec((1, SSM_WIDTH)),
        ],
        out_specs=(pl.BlockSpec((1, rows_in, SSM_WIDTH), row),
                   pl.BlockSpec((1, SSM_WIDTH, SSM_STATE), per_b)),
        out_shape=(jax.ShapeDtypeStruct((b, l, SSM_WIDTH), F32),
                   jax.ShapeDtypeStruct((b, SSM_WIDTH, SSM_STATE), F32)),
        scratch_shapes=[pltpu.VMEM((SSM_WIDTH, SSM_STATE), F32), pltpu.VMEM((8, CONV_CH), F32)],
        compiler_params=_params("arbitrary", "arbitrary"),
        name="ssd",
    )(xbc, z, dt, conv_prev8, h0, conv_w8, conv_b, dt_bias, a_log, d_skip, g_ssm)


def _post1_body(*refs, n_branch):
    x_ref = refs[0]
    att_refs = refs[1:1 + 2 * n_branch] if n_branch > 1 else refs[1:2]
    y_ref, wout_ref, g_ref, wxq_ref, x1_ref, qx_ref = refs[len(att_refs) + 1:]
    if n_branch > 1:
        outs = [att_refs[2 * i][...] for i in range(n_branch)]
        lses = [att_refs[2 * i + 1][...] for i in range(n_branch)]
        m = functools.reduce(jnp.maximum, lses)
        es = [jnp.exp(l - m) for l in lses]
        att = sum(e * o for e, o in zip(es, outs)) / sum(es)
    else:
        att = att_refs[0][...]
    mixed = jnp.concatenate([att, y_ref[...]], axis=1).astype(BF16)
    x1 = x_ref[...] + _dot(mixed, wout_ref[...])
    x1_ref[...] = x1
    qx_ref[...] = _dot(_rmsnorm(x1, g_ref[...]).astype(BF16), wxq_ref[...]).astype(BF16)


def _post1(x, att_parts, y, w_out, g_xatt, w_xq, tm):
    n, _ = x.shape
    n_branch = len(att_parts) // 2 if len(att_parts) > 1 else 1
    row = lambda i: (i, 0)
    half = pl.BlockSpec((tm, ATT_WIDTH), row)
    full = pl.BlockSpec((tm, D_MODEL), row)
    return pl.pallas_call(
        functools.partial(_post1_body, n_branch=n_branch),
        grid=(n // tm,),
        in_specs=[full] + [half] * len(att_parts) + [half, _const_spec(w_out.shape), _const_spec((1, D_MODEL)),
                                                     _const_spec(w_xq.shape)],
        out_specs=(full, full),
        out_shape=(jax.ShapeDtypeStruct((n, D_MODEL), F32), jax.ShapeDtypeStruct((n, D_MODEL), BF16)),
        compiler_params=_params("arbitrary"),
        name="post1",
    )(x, *att_parts, y, w_out, g_xatt, w_xq)


def _mem_kv_body(m_ref, g_ref, wk_ref, wv_ref, k_ref, v_ref):
    h = _rmsnorm(m_ref[...], g_ref[...]).astype(BF16)
    k_ref[...] = _dot(h, wk_ref[...])
    v_ref[...] = _dot(h, wv_ref[...])


def _mem_kv(mem, g_mem, w_mk, w_mv, tm):
    n, _ = mem.shape
    row = pl.BlockSpec((tm, D_MODEL), lambda i: (i, 0))
    return pl.pallas_call(
        _mem_kv_body,
        grid=(n // tm,),
        in_specs=[row, _const_spec((1, D_MODEL)), _const_spec(w_mk.shape), _const_spec(w_mv.shape)],
        out_specs=(row, row),
        out_shape=(jax.ShapeDtypeStruct((n, D_MODEL), F32), jax.ShapeDtypeStruct((n, D_MODEL), F32)),
        compiler_params=_params("arbitrary"),
        name="mem_kv",
    )(mem, g_mem, w_mk, w_mv)


def _xattn_body(q_ref, mk_ref, mv_ref, o_ref):
    scale = XATT_HEAD_DIM ** -0.5
    for gi in range(q_ref.shape[0]):
        for hd in range(N_XATT_HEADS):
            sl = slice(hd * XATT_HEAD_DIM, (hd + 1) * XATT_HEAD_DIM)
            s = _dot_nt(q_ref[gi, :, sl], mk_ref[gi, :, sl].astype(BF16)) * scale
            p = jnp.exp(s - jnp.max(s, axis=-1, keepdims=True))
            den = jnp.sum(p, axis=-1, keepdims=True)
            o = _dot(p.astype(BF16), mv_ref[gi, :, sl].astype(BF16)) / den
            o_ref[gi, :, sl] = o.astype(BF16)


def _xattn(qx, mem_k, mem_v, gb, tm):
    b, l, _ = qx.shape
    qspec = pl.BlockSpec((gb, tm, D_MODEL), lambda bi, j: (bi, j, 0))
    mspec = pl.BlockSpec((gb, N_MEM, D_MODEL), lambda bi, j: (bi, 0, 0))
    return pl.pallas_call(
        _xattn_body,
        grid=(b // gb, l // tm),
        in_specs=[qspec, mspec, mspec],
        out_specs=qspec,
        out_shape=jax.ShapeDtypeStruct((b, l, D_MODEL), BF16),
        compiler_params=_params("arbitrary", "arbitrary"),
        name="xattn",
    )(qx, mem_k, mem_v)


def _post2_body(x1_ref, o_ref, wxo_ref, gmlp_ref, wup_ref, wdown_ref, gfin_ref, y_ref):
    x2 = x1_ref[...] + _dot(o_ref[...], wxo_ref[...])
    hm = _rmsnorm(x2, gmlp_ref[...]).astype(BF16)
    acc = x2
    for c in range(D_FF // D_MODEL):
        sl = slice(c * D_MODEL, (c + 1) * D_MODEL)
        u = jnp.maximum(_dot(hm, wup_ref[:, sl]), 0.0)
        acc = acc + _dot((u * u).astype(BF16), wdown_ref[sl, :])
    y_ref[...] = _rmsnorm(acc, gfin_ref[...])


def _post2(x1, o, w_xo, g_mlp, w_up, w_down, g_final, tm):
    n, _ = x1.shape
    row = pl.BlockSpec((tm, D_MODEL), lambda i: (i, 0))
    return pl.pallas_call(
        _post2_body,
        grid=(n // tm,),
        in_specs=[row, row, _const_spec(w_xo.shape), _const_spec((1, D_MODEL)), _const_spec(w_up.shape),
                  _const_spec(w_down.shape), _const_spec((1, D_MODEL))],
        out_specs=row,
        out_shape=jax.ShapeDtypeStruct((n, D_MODEL), F32),
        compiler_params=_params("arbitrary"),
        name="post2",
    )(x1, o, w_xo, g_mlp, w_up, w_down, g_final)


def _rope_tables(pos):
    half = HEAD_DIM // 2
    inv = ROPE_THETA ** (-jnp.arange(half, dtype=F32) * 2.0 / HEAD_DIM)
    ang = pos.astype(F32)[:, None] * inv[None, :]
    c, s = jnp.cos(ang), jnp.sin(ang)
    return jnp.concatenate([c, c, c, c], axis=1), jnp.concatenate([-s, s, -s, s], axis=1)


def _pad_lanes(t):
    return jnp.pad(t, ((0, 0), (0, LANES - t.shape[1])))


def kernel(x_prompt, x_sample, cache_win_k, cache_win_v, state_conv, state_ssm, cache_mem_k, cache_mem_v,
           mem_prompt, g_mix, w_in, conv_w, conv_b, dt_bias, a_log, d_skip, g_ssm, w_out, g_xatt, g_mem,
           w_xq, w_mk, w_mv, w_xo, g_mlp, w_up, w_down, g_final):
    depth = w_in.shape[0]
    assert depth == 1, "kernel is written for the single-layer trunk of this problem"
    bp, s_len, _ = x_prompt.shape
    bs, t_new, _ = x_sample.shape
    lb = cache_win_k.shape[2]
    lw = min(MAX_WINDOW, s_len)
    n_proj = ATT_WIDTH * 3 + SSM_WIDTH + CONV_CH

    li = 0
    row = lambda t: t[li].reshape(1, -1)
    w_main = w_in[li, :, :n_proj].astype(BF16)
    w_dt = _pad_lanes(w_in[li, :, n_proj:]).astype(BF16)
    conv_w8 = jnp.pad(conv_w[li], ((0, 8 - CONV_WIDTH), (0, 0)))
    dt_b = _pad_lanes(row(dt_bias))
    a_lg = _pad_lanes(row(a_log))
    d_sk = jnp.repeat(d_skip[li], HEAD_DIM).reshape(1, -1)
    w_o, w_q, w_k, w_v, w_x = (t[li].astype(BF16) for t in (w_out, w_xq, w_mk, w_mv, w_xo))
    w_u, w_d = w_up[li].astype(BF16), w_down[li].astype(BF16)
    g_fin = g_final.reshape(1, -1)

    cos_p, sin_p = _rope_tables(jnp.arange(s_len, dtype=jnp.int32))
    q, k, v, k32, v32, z, xbc, dt = _in_proj(x_prompt, row(g_mix), w_main, w_dt, cos_p, sin_p, lw, 256)
    parts = []
    for _, dil in DILATED_BRANCHES:
        parts.extend(t.reshape(bp * s_len, ATT_WIDTH) for t in _attn_branch(q, k, v, dil))
    y_ssm, ssm_p = _ssd(xbc, z, dt, jnp.zeros((bp, 8, CONV_CH), F32),
                        jnp.zeros((bp, SSM_WIDTH, SSM_STATE), F32), conv_w8, row(conv_b), dt_b, a_lg, d_sk,
                        row(g_ssm), s_len)
    mk, mv = _mem_kv(mem_prompt.reshape(bp * N_MEM, D_MODEL), row(g_mem), w_k, w_v, 256)
    x1, qx = _post1(x_prompt.reshape(bp * s_len, D_MODEL), parts, y_ssm.reshape(bp * s_len, SSM_WIDTH),
                    w_o, row(g_xatt), w_q, 256)
    o = _xattn(qx.reshape(bp, s_len, D_MODEL), mk.reshape(bp, N_MEM, D_MODEL), mv.reshape(bp, N_MEM, D_MODEL),
               1, 512)
    y_prompt = _post2(x1, o.reshape(bp * s_len, D_MODEL), w_x, row(g_mlp), w_u, w_d, g_fin, 256)
    y_prompt = y_prompt.reshape(bp, s_len, D_MODEL)
    win_k_p = k32.reshape(1, bp, lw, N_ATT_HEADS, HEAD_DIM)
    win_v_p = v32.reshape(1, bp, lw, N_ATT_HEADS, HEAD_DIM)
    conv_p = xbc[:, s_len - (CONV_WIDTH - 1):][None]
    ssm_p = ssm_p.reshape(1, bp, N_SSM_HEADS, HEAD_DIM, SSM_STATE)
    mk_p = mk.reshape(1, bp, N_MEM, N_XATT_HEADS, XATT_HEAD_DIM)
    mv_p = mv.reshape(1, bp, N_MEM, N_XATT_HEADS, XATT_HEAD_DIM)

    r = SAMPLE_ROWS
    xs_pad = jnp.pad(x_sample, ((0, 0), (0, r - t_new), (0, 0))).reshape(1, bs * r, D_MODEL)
    pos_s = PAST_LEN + (jnp.arange(bs * r, dtype=jnp.int32) % r)
    cos_s, sin_s = _rope_tables(pos_s)
    q, _, _, k32, v32, z, xbc, dt = _in_proj(xs_pad, row(g_mix), w_main, w_dt, cos_s, sin_s, bs * r, 256)
    per_seq = lambda t: t.reshape(bs, r, t.shape[-1])
    att, win_k_s, win_v_s = _attn_sample(per_seq(q), per_seq(k32), per_seq(v32),
                                         cache_win_k[li].reshape(bs, lb, ATT_WIDTH),
                                         cache_win_v[li].reshape(bs, lb, ATT_WIDTH), t_new)
    conv_prev8 = jnp.pad(state_conv[li], ((0, 0), (8 - (CONV_WIDTH - 1), 0), (0, 0)))
    xbc_s = per_seq(xbc)
    y_ssm, ssm_s = _ssd(xbc_s, per_seq(z), per_seq(dt), conv_prev8,
                        state_ssm[li].reshape(bs, SSM_WIDTH, SSM_STATE), conv_w8, row(conv_b), dt_b, a_lg, d_sk,
                        row(g_ssm), t_new)
    x1, qx = _post1(xs_pad.reshape(bs * r, D_MODEL), [att.reshape(bs * r, ATT_WIDTH)],
                    y_ssm.reshape(bs * r, SSM_WIDTH), w_o, row(g_xatt), w_q, 256)
    o = _xattn(qx.reshape(bs, r, D_MODEL), cache_mem_k[li].reshape(bs, N_MEM, D_MODEL),
               cache_mem_v[li].reshape(bs, N_MEM, D_MODEL), 4, r)
    y_s = _post2(x1, o.reshape(bs * r, D_MODEL), w_x, row(g_mlp), w_u, w_d, g_fin, 256)
    y_sample = y_s.reshape(bs, r, D_MODEL)[:, :t_new]
    win_k_s = win_k_s.reshape(1, bs, lb, N_ATT_HEADS, HEAD_DIM)
    win_v_s = win_v_s.reshape(1, bs, lb, N_ATT_HEADS, HEAD_DIM)
    conv_s = jnp.concatenate([state_conv[li], xbc_s[:, :t_new]], axis=1)[:, -(CONV_WIDTH - 1):][None]
    ssm_s = ssm_s.reshape(1, bs, N_SSM_HEADS, HEAD_DIM, SSM_STATE)

    return (y_prompt, y_sample, win_k_p, win_v_p, conv_p, ssm_p, mk_p, mv_p,
            win_k_s, win_v_s, conv_s, ssm_s)
```

```python
import functools

import jax
import jax.numpy as jnp
from jax import lax
from jax.experimental import pallas as pl
from jax.experimental.pallas import tpu as pltpu

F32 = jnp.float32
BF16 = jnp.bfloat16

D_MODEL = 1024
PAST_LEN = 8192
HEAD_DIM = 64
ATT_WIDTH = 512
N_ATT_HEADS = 8
DILATED_BRANCHES = ((128, 1), (512, 4), (2048, 16))
MAX_WINDOW = 2048
ROPE_THETA = 10000.0
SSM_WIDTH = 512
N_SSM_HEADS = 8
SSM_GROUPS = 2
SSM_STATE = 128
CONV_WIDTH = 4
SSD_CHUNK = 128
CONV_CH = 1024
N_MEM = 256
N_XATT_HEADS = 4
XATT_HEAD_DIM = 256
D_FF = 4096
EPS = 1e-6

LANES = 128
PAIR = 2 * HEAD_DIM
N_PAIRS = ATT_WIDTH // PAIR
BRANCH_BLOCK = 128
ATT_TILE = 2048
UNITS_PER_ITER = 4
SAMPLE_ROWS = 16
VMEM_LIMIT = 56 * 1024 * 1024


def _params(*sem):
    return pltpu.CompilerParams(dimension_semantics=sem, vmem_limit_bytes=VMEM_LIMIT)


def _const_spec(shape):
    return pl.BlockSpec(shape, lambda *_: (0,) * len(shape), pipeline_mode=pl.Buffered(1))


def _rmsnorm(x, g):
    return x * lax.rsqrt(jnp.mean(x * x, axis=-1, keepdims=True) + EPS) * g


def _dot(a, b):
    return jnp.dot(a, b, preferred_element_type=F32)


def _dot_nt(a, b):
    return lax.dot_general(a, b, (((1,), (1,)), ((), ())), preferred_element_type=F32)


def _dot_tn(a, b):
    return lax.dot_general(a, b, (((0,), (0,)), ((), ())), preferred_element_type=F32)


def _in_proj_body(x_ref, g_ref, w_ref, wdt_ref, cos_ref, sin_ref,
                  q_ref, k_ref, v_ref, k32_ref, v32_ref, z_ref, xbc_ref, dt_ref):
    h = _rmsnorm(x_ref[0], g_ref[...]).astype(BF16)
    tm = h.shape[0]
    cos = jnp.concatenate([cos_ref[...]] * N_PAIRS, axis=1)
    sin = jnp.concatenate([sin_ref[...]] * N_PAIRS, axis=1)
    lane = lax.broadcasted_iota(jnp.int32, (tm, ATT_WIDTH), 1)
    first_half = (lane & (HEAD_DIM // 2)) == 0

    def rope(t):
        partner = jnp.where(first_half,
                            pltpu.roll(t, ATT_WIDTH - HEAD_DIM // 2, axis=1),
                            pltpu.roll(t, HEAD_DIM // 2, axis=1))
        return t * cos + partner * sin

    q = rope(_dot(h, w_ref[:, 0:512]))
    k = rope(_dot(h, w_ref[:, 512:1024]))
    v = _dot(h, w_ref[:, 1024:1536])
    for pr in range(N_PAIRS):
        sl = slice(pr * PAIR, (pr + 1) * PAIR)
        q_ref[0, pr] = q[:, sl]
        k_ref[0, pr] = k[:, sl]
        v_ref[0, pr] = v[:, sl]
    k32_ref[0] = k
    v32_ref[0] = v
    z_ref[0] = _dot(h, w_ref[:, 1536:2048])
    xbc_ref[0] = _dot(h, w_ref[:, 2048:3072])
    dt_ref[0] = _dot(h, wdt_ref[...])


def _in_proj(x, g, w_main, w_dt, cos, sin, window, tm):
    b, l, _ = x.shape
    nj = l // tm
    j0 = (l - window) // tm
    row = lambda bi, j: (bi, j, 0)
    slab = lambda bi, j: (bi, 0, j, 0)
    win = lambda bi, j: (bi, jnp.maximum(j - j0, 0), 0)
    slab_shape = jax.ShapeDtypeStruct((b, N_PAIRS, l, PAIR), F32)
    slab_spec = pl.BlockSpec((1, N_PAIRS, tm, PAIR), slab)
    outs = (
        slab_shape, slab_shape, slab_shape,
        jax.ShapeDtypeStruct((b, window, ATT_WIDTH), F32),
        jax.ShapeDtypeStruct((b, window, ATT_WIDTH), F32),
        jax.ShapeDtypeStruct((b, l, SSM_WIDTH), F32),
        jax.ShapeDtypeStruct((b, l, CONV_CH), F32),
        jax.ShapeDtypeStruct((b, l, LANES), F32),
    )
    return pl.pallas_call(
        _in_proj_body,
        grid=(b, nj),
        in_specs=[
            pl.BlockSpec((1, tm, D_MODEL), row),
            _const_spec((1, D_MODEL)),
            _const_spec(w_main.shape),
            _const_spec(w_dt.shape),
            pl.BlockSpec((tm, LANES), lambda bi, j: (j, 0)),
            pl.BlockSpec((tm, LANES), lambda bi, j: (j, 0)),
        ],
        out_specs=(
            slab_spec, slab_spec, slab_spec,
            pl.BlockSpec((1, tm, ATT_WIDTH), win),
            pl.BlockSpec((1, tm, ATT_WIDTH), win),
            pl.BlockSpec((1, tm, SSM_WIDTH), row),
            pl.BlockSpec((1, tm, CONV_CH), row),
            pl.BlockSpec((1, tm, LANES), row),
        ),
        out_shape=outs,
        compiler_params=_params("arbitrary", "arbitrary"),
        name="in_proj",
    )(x, g, w_main, w_dt, cos, sin)


def _attn_prompt_body(q_ref, kc_ref, kp_ref, vc_ref, vp_ref, o_ref,
                      kall_ref, vall_ref, r0_ref, r1_ref, m0_ref, m1_ref):
    t = pl.program_id(2)
    n = BRANCH_BLOCK
    tile = ATT_TILE
    kall_ref[0:tile, :] = kp_ref[...]
    kall_ref[tile:, :] = kc_ref[...]
    vall_ref[0:tile, :] = vp_ref[...]
    vall_ref[tile:, :] = vc_ref[...]

    row = lax.broadcasted_iota(jnp.int32, (2 * n, 2 * n), 0) & (n - 1)
    col = lax.broadcasted_iota(jnp.int32, (2 * n, 2 * n), 1)
    bias_cur = jnp.where(col - n <= row, 0.0, -jnp.inf)
    bias = jnp.where(col < n, jnp.where(col >= row, 0.0, -jnp.inf), bias_cur)
    bias_first = jnp.where(col < n, -jnp.inf, bias_cur)
    low = lax.broadcasted_iota(jnp.int32, (n, PAIR), 1) < HEAD_DIM
    low2 = lax.broadcasted_iota(jnp.int32, (2 * n, PAIR), 1) < HEAD_DIM
    scale = HEAD_DIM ** -0.5

    def rows(start, stride):
        return pl.ds(start, n) if stride == 1 else pl.ds(start, n, stride=stride)

    def run_units(units, stride, merge):
        loaded = []
        for q0, _ in units:
            qv = q_ref[rows(q0, stride), :]
            kk = jnp.concatenate([kall_ref[rows(tile + q0 - n * stride, stride), :],
                                  kall_ref[rows(tile + q0, stride), :]], axis=0)
            vv = jnp.concatenate([vall_ref[rows(tile + q0 - n * stride, stride), :],
                                  vall_ref[rows(tile + q0, stride), :]], axis=0)
            qm = jnp.concatenate([jnp.where(low, qv, 0.0), jnp.where(low, 0.0, qv)], axis=0).astype(BF16)
            loaded.append((qm, kk.astype(BF16),
                           jnp.where(low2, vv, 1.0).astype(BF16), jnp.where(low2, 1.0, vv).astype(BF16)))
        scores = []
        for (qm, kk, _, _), (_, has_prev) in zip(loaded, units):
            b = bias if has_prev is None else jnp.where(has_prev, bias, bias_first)
            scores.append(_dot_nt(qm, kk) * scale + b)
        m_new, alpha = [], []
        for s, (q0, _) in zip(scores, units):
            m_row = jnp.max(s, axis=-1, keepdims=True)
            if merge:
                m_old = jnp.concatenate([m0_ref[rows(q0, stride), :], m1_ref[rows(q0, stride), :]], axis=0)
                m = jnp.maximum(m_old, m_row)
                alpha.append(jnp.exp(m_old - m))
            else:
                m = jnp.broadcast_to(m_row, (2 * n, PAIR))
            m_new.append(m)
        probs = [jnp.exp(s - jnp.concatenate([m, m], axis=1)).astype(BF16) for s, m in zip(scores, m_new)]
        for i, (q0, _) in enumerate(units):
            _, _, va0, va1 = loaded[i]
            acc0 = _dot(probs[i][0:n], va0)
            acc1 = _dot(probs[i][n:], va1)
            if merge:
                acc0 = acc0 + alpha[i][0:n] * r0_ref[rows(q0, stride), :]
                acc1 = acc1 + alpha[i][n:] * r1_ref[rows(q0, stride), :]
            r0_ref[rows(q0, stride), :] = acc0
            r1_ref[rows(q0, stride), :] = acc1
            m0_ref[rows(q0, stride), :] = m_new[i][0:n]
            m1_ref[rows(q0, stride), :] = m_new[i][n:]

    g_units = UNITS_PER_ITER
    not_first_tile = t > 0

    def branch_d1(it, carry):
        base = pl.multiple_of(it * (g_units * n), g_units * n)
        units = [(base + g * n, jnp.logical_or(not_first_tile, it > 0) if g == 0 else None)
                 for g in range(g_units)]
        run_units(units, 1, merge=False)
        return carry

    lax.fori_loop(0, tile // (g_units * n), branch_d1, 0)

    def branch_d4(r, carry):
        units = [(i * (4 * n) + r, not_first_tile if i == 0 else None) for i in range(tile // (4 * n))]
        run_units(units, 4, merge=True)
        return carry

    lax.fori_loop(0, 4, branch_d4, 0)

    def branch_d16(it, carry):
        units = [(it * g_units + g, not_first_tile) for g in range(g_units)]
        run_units(units, 16, merge=True)
        return carry

    lax.fori_loop(0, 16 // g_units, branch_d16, 0)

    def finish(c, carry):
        r0 = pl.multiple_of(c * n, n)
        acc0 = r0_ref[pl.ds(r0, n), :]
        acc1 = r1_ref[pl.ds(r0, n), :]
        num = jnp.where(low, acc0, acc1)
        den = pltpu.roll(jnp.where(low, acc1, acc0), HEAD_DIM, axis=1)
        o_ref[pl.ds(r0, n), :] = num / den
        return carry

    lax.fori_loop(0, tile // n, finish, 0)


def _attn_prompt(q, k, v):
    b, npair, s, w = q.shape
    assert [d for _, d in DILATED_BRANCHES] == [1, 4, 16] and s % ATT_TILE == 0
    tile = ATT_TILE
    cur = pl.BlockSpec((None, None, tile, w), lambda bi, p, t: (bi, p, t, 0))
    prev = pl.BlockSpec((None, None, tile, w), lambda bi, p, t: (bi, p, jnp.maximum(t - 1, 0), 0))
    return pl.pallas_call(
        _attn_prompt_body,
        grid=(b, npair, s // tile),
        in_specs=[cur, cur, prev, cur, prev],
        out_specs=cur,
        out_shape=jax.ShapeDtypeStruct((b, npair, s, w), F32),
        scratch_shapes=[pltpu.VMEM((2 * tile, w), F32), pltpu.VMEM((2 * tile, w), F32)]
        + [pltpu.VMEM((tile, w), F32)] * 4,
        compiler_params=_params("arbitrary", "arbitrary", "arbitrary"),
        name="attn_prompt",
    )(q, k, k, v, v)


def _branch_count(dist):
    cnt = jnp.zeros(dist.shape, F32)
    for window, dil in DILATED_BRANCHES:
        hit = (dist >= 0) & (dist <= window) & ((dist & (dil - 1)) == 0)
        cnt = cnt + hit.astype(F32)
    return cnt


def _attn_sample_body(q_ref, kn_ref, vn_ref, kt_ref, vt_ref, att_ref, kto_ref, vto_ref, *, n_new):
    lb = kt_ref.shape[1]
    rows = q_ref.shape[1]
    k_new = kn_ref[...]
    v_new = vn_ref[...]

    tail_lane = lax.broadcasted_iota(jnp.int32, (64, LANES), 1) >= LANES - n_new
    for new, src_ref, dst_ref in ((k_new, kt_ref, kto_ref), (v_new, vt_ref, vto_ref)):
        last8 = pltpu.roll(new[0:8], 8 - n_new, axis=0)
        padded = jnp.concatenate([jnp.zeros((LANES - 8, ATT_WIDTH), F32), last8], axis=0)
        new_t = jnp.concatenate([padded[:, p * LANES:(p + 1) * LANES].T for p in range(N_PAIRS)], axis=0)
        for c in range(ATT_WIDTH // 64):
            shifted = pltpu.roll(src_ref[c * 64:(c + 1) * 64, :], lb - n_new, axis=1)
            dst_ref[c * 64:(c + 1) * 64, 0:lb - LANES] = shifted[:, 0:lb - LANES]
            dst_ref[c * 64:(c + 1) * 64, lb - LANES:lb] = jnp.where(
                tail_lane, new_t[c * 64:(c + 1) * 64], shifted[:, lb - LANES:lb])

    t_c = lax.broadcasted_iota(jnp.int32, (rows, lb), 0)
    i_c = lax.broadcasted_iota(jnp.int32, (rows, lb), 1)
    cnt_c = _branch_count(lb + t_c - i_c)
    t_n = lax.broadcasted_iota(jnp.int32, (rows, rows), 0)
    i_n = lax.broadcasted_iota(jnp.int32, (rows, rows), 1)
    cnt_n = jnp.where(i_n < n_new, _branch_count(t_n - i_n), 0.0)
    low = lax.broadcasted_iota(jnp.int32, (rows, PAIR), 1) < HEAD_DIM
    scale = HEAD_DIM ** -0.5
    for pr in range(N_PAIRS):
        sl = slice(pr * PAIR, (pr + 1) * PAIR)
        qp = q_ref[pr]
        ktp = kt_ref[sl, :].astype(BF16)
        vtp = vt_ref[sl, :].astype(BF16)
        knp = k_new[:, sl].astype(BF16)
        vnp = v_new[:, sl].astype(BF16)
        o_h = []
        for hh in range(2):
            qm = jnp.where(low if hh == 0 else jnp.logical_not(low), qp, 0.0).astype(BF16)
            s_c = jnp.where(cnt_c > 0, _dot(qm, ktp) * scale, -jnp.inf)
            s_n = jnp.where(cnt_n > 0, _dot_nt(qm, knp) * scale, -jnp.inf)
            m = jnp.maximum(jnp.max(s_c, axis=-1, keepdims=True), jnp.max(s_n, axis=-1, keepdims=True))
            p_c = cnt_c * jnp.exp(s_c - m)
            p_n = cnt_n * jnp.exp(s_n - m)
            den = jnp.sum(p_c, axis=-1, keepdims=True) + jnp.sum(p_n, axis=-1, keepdims=True)
            pv = _dot_nt(p_c.astype(BF16), vtp) + _dot(p_n.astype(BF16), vnp)
            o_h.append(pv / den)
        att_ref[pr] = jnp.where(low, o_h[0], o_h[1])


def _attn_sample(q, k_new, v_new, kt_cache, vt_cache, rows, n_new):
    b, w, lb = kt_cache.shape
    assert n_new <= 8 and lb % LANES == 0
    slab = pl.BlockSpec((None, N_PAIRS, rows, PAIR), lambda bi: (0, 0, bi, 0))
    new = pl.BlockSpec((None, rows, w), lambda bi: (0, bi, 0))
    cache = pl.BlockSpec((None, w, lb), lambda bi: (bi, 0, 0))
    return pl.pallas_call(
        functools.partial(_attn_sample_body, n_new=n_new),
        grid=(b,),
        in_specs=[slab, new, new, cache, cache],
        out_specs=(slab, cache, cache),
        out_shape=(jax.ShapeDtypeStruct(q.shape, F32),
                   jax.ShapeDtypeStruct((b, w, lb), F32),
                   jax.ShapeDtypeStruct((b, w, lb), F32)),
        compiler_params=_params("arbitrary"),
        name="attn_sample",
    )(q, k_new, v_new, kt_cache, vt_cache)


def _ssd_body(xbc_ref, z_ref, dt_ref, cprev_ref, h0_ref, cw_ref, cb_ref, dtb_ref, alog_ref, dskip_ref, g_ref,
              y_ref, hout_ref, h_ref, xprev_ref, *, rows_in, n_valid):
    c = pl.program_id(1)
    q = SSD_CHUNK

    @pl.when(c == 0)
    def _():
        h_ref[...] = h0_ref[0]
        xprev_ref[...] = cprev_ref[0]

    def chunk_rows(ref):
        t = ref[0]
        if rows_in < q:
            t = jnp.concatenate([t, jnp.zeros((q - rows_in, t.shape[1]), t.dtype)], axis=0)
        return t

    cur = chunk_rows(xbc_ref)
    prev8 = xprev_ref[...]
    i8 = lax.broadcasted_iota(jnp.int32, (8, CONV_CH), 0)
    acc = cb_ref[...] + cur * cw_ref[CONV_WIDTH - 1:CONV_WIDTH, :]
    for sh in range(1, CONV_WIDTH):
        rolled = pltpu.roll(cur, sh, axis=0)
        head = jnp.where(i8 < sh, pltpu.roll(prev8, sh, axis=0), rolled[0:8])
        shifted = jnp.concatenate([head, rolled[8:]], axis=0)
        acc = acc + shifted * cw_ref[CONV_WIDTH - 1 - sh:CONV_WIDTH - sh, :]
    xprev_ref[...] = cur[q - 8:q]
    conv = acc * (1.0 / (1.0 + jnp.exp(-acc)))
    xs = conv[:, 0:SSM_WIDTH]

    row_l = lax.broadcasted_iota(jnp.int32, (q, LANES), 0)
    lane_l = lax.broadcasted_iota(jnp.int32, (q, LANES), 1)
    dt_in = chunk_rows(dt_ref) + dtb_ref[...]
    dtv = jnp.maximum(dt_in, 0.0) + jnp.log1p(jnp.exp(-jnp.abs(dt_in)))
    dtv = jnp.where((c * q + row_l < n_valid) & (lane_l < N_SSM_HEADS), dtv, 0.0)
    a = dtv * (-jnp.exp(alog_ref[...]))
    tri = row_l >= lane_l
    a_cs = jnp.dot(tri.astype(F32), a, preferred_element_type=F32, precision=lax.Precision.HIGHEST)
    a_cs_t = a_cs.T
    low = lax.broadcasted_iota(jnp.int32, (q, PAIR), 1) < HEAD_DIM

    def per_pair(col0, col1):
        return jnp.where(low, col0, col1)

    ys = []
    for pr in range(N_SSM_HEADS // 2):
        grp = (2 * pr) // (N_SSM_HEADS // SSM_GROUPS)
        sl = slice(pr * PAIR, (pr + 1) * PAIR)
        bm = conv[:, SSM_WIDTH + grp * SSM_STATE:SSM_WIDTH + (grp + 1) * SSM_STATE]
        cm = conv[:, SSM_WIDTH + (SSM_GROUPS + grp) * SSM_STATE:SSM_WIDTH + (SSM_GROUPS + grp + 1) * SSM_STATE]
        bmb = bm.astype(BF16)
        cmb = cm.astype(BF16)
        cb = _dot_nt(cmb, bmb)
        h0, h1 = 2 * pr, 2 * pr + 1
        cs0, cs1 = a_cs[:, h0:h0 + 1], a_cs[:, h1:h1 + 1]
        tot0, tot1 = a_cs[q - 1:q, h0:h0 + 1], a_cs[q - 1:q, h1:h1 + 1]
        xs_p = xs[:, sl]
        xdt = xs_p * per_pair(dtv[:, h0:h0 + 1], dtv[:, h1:h1 + 1])
        xdt_b = xdt.astype(BF16)
        y_d = []
        for hh, cs in ((h0, cs0), (h1, cs1)):
            decay = jnp.exp(jnp.where(tri, cs - a_cs_t[hh:hh + 1, :], -jnp.inf))
            y_d.append(_dot((cb * decay).astype(BF16), xdt_b))
        y_diag = per_pair(y_d[0], y_d[1])
        to_end = per_pair(jnp.exp(tot0 - cs0), jnp.exp(tot1 - cs1))
        states = _dot_tn((xdt * to_end).astype(BF16), bmb)
        h_pair = h_ref[pr * PAIR:(pr + 1) * PAIR, :]
        y_off = _dot_nt(cmb, h_pair.astype(BF16)) * per_pair(jnp.exp(cs0), jnp.exp(cs1))
        chunk_decay = jnp.concatenate([jnp.broadcast_to(jnp.exp(tot0), (HEAD_DIM, SSM_STATE)),
                                       jnp.broadcast_to(jnp.exp(tot1), (HEAD_DIM, SSM_STATE))], axis=0)
        h_ref[pr * PAIR:(pr + 1) * PAIR, :] = h_pair * chunk_decay + states
        ys.append(y_diag + y_off + dskip_ref[:, sl] * xs_p)
    y = jnp.concatenate(ys, axis=1)

    zz = chunk_rows(z_ref)
    u = y * (zz * (1.0 / (1.0 + jnp.exp(-zz))))
    gw = SSM_WIDTH // SSM_GROUPS
    normed = []
    for grp in range(SSM_GROUPS):
        ug = u[:, grp * gw:(grp + 1) * gw]
        normed.append(ug * lax.rsqrt(jnp.mean(ug * ug, axis=-1, keepdims=True) + EPS))
    out = jnp.concatenate(normed, axis=1) * g_ref[...]
    y_ref[0] = out[0:rows_in]
    hout_ref[0] = h_ref[...]


def _ssd(xbc, z, dt, conv_prev8, h0, conv_w8, conv_b, dt_bias, a_log, d_skip, g_ssm, n_valid):
    b, l, _ = xbc.shape
    rows_in = min(SSD_CHUNK, l)
    nc = l // rows_in
    row = lambda bi, c: (bi, c, 0)
    per_b = lambda bi, c: (bi, 0, 0)
    return pl.pallas_call(
        functools.partial(_ssd_body, rows_in=rows_in, n_valid=n_valid),
        grid=(b, nc),
        in_specs=[
            pl.BlockSpec((1, rows_in, CONV_CH), row),
            pl.BlockSpec((1, rows_in, SSM_WIDTH), row),
            pl.BlockSpec((1, rows_in, LANES), row),
            pl.BlockSpec((1, 8, CONV_CH), per_b),
            pl.BlockSpec((1, SSM_WIDTH, SSM_STATE), per_b),
            _const_spec((8, CONV_CH)),
            _const_spec((1, CONV_CH)),
            _const_spec((1, LANES)),
            _const_spec((1, LANES)),
            _const_spec((1, SSM_WIDTH)),
            _const_spec((1, SSM_WIDTH)),
        ],
        out_specs=(pl.BlockSpec((1, rows_in, SSM_WIDTH), row),
                   pl.BlockSpec((1, SSM_WIDTH, SSM_STATE), per_b)),
        out_shape=(jax.ShapeDtypeStruct((b, l, SSM_WIDTH), F32),
                   jax.ShapeDtypeStruct((b, SSM_WIDTH, SSM_STATE), F32)),
        scratch_shapes=[pltpu.VMEM((SSM_WIDTH, SSM_STATE), F32), pltpu.VMEM((8, CONV_CH), F32)],
        compiler_params=_params("arbitrary", "arbitrary"),
        name="ssd",
    )(xbc, z, dt, conv_prev8, h0, conv_w8, conv_b, dt_bias, a_log, d_skip, g_ssm)


def _post1_body(x_ref, att_ref, y_ref, wout_ref, g_ref, wxq_ref, x1_ref, qx_ref):
    mixed = jnp.concatenate([att_ref[0, pr] for pr in range(N_PAIRS)] + [y_ref[0]], axis=1).astype(BF16)
    x1 = x_ref[0] + _dot(mixed, wout_ref[...])
    x1_ref[0] = x1
    qx_ref[0] = _dot(_rmsnorm(x1, g_ref[...]).astype(BF16), wxq_ref[...]).astype(BF16)


def _post1(x, att, y, w_out, g_xatt, w_xq, tm):
    b, l, _ = x.shape
    row = lambda bi, j: (bi, j, 0)
    full = pl.BlockSpec((1, tm, D_MODEL), row)
    return pl.pallas_call(
        _post1_body,
        grid=(b, l // tm),
        in_specs=[full,
                  pl.BlockSpec((1, N_PAIRS, tm, PAIR), lambda bi, j: (bi, 0, j, 0)),
                  pl.BlockSpec((1, tm, SSM_WIDTH), row),
                  _const_spec(w_out.shape), _const_spec((1, D_MODEL)), _const_spec(w_xq.shape)],
        out_specs=(full, full),
        out_shape=(jax.ShapeDtypeStruct((b, l, D_MODEL), F32), jax.ShapeDtypeStruct((b, l, D_MODEL), BF16)),
        compiler_params=_params("arbitrary", "arbitrary"),
        name="post1",
    )(x, att, y, w_out, g_xatt, w_xq)


def _mem_kv_body(m_ref, g_ref, wk_ref, wv_ref, k_ref, v_ref):
    h = _rmsnorm(m_ref[...], g_ref[...]).astype(BF16)
    k_ref[...] = _dot(h, wk_ref[...])
    v_ref[...] = _dot(h, wv_ref[...])


def _mem_kv(mem, g_mem, w_mk, w_mv, tm):
    n, _ = mem.shape
    row = pl.BlockSpec((tm, D_MODEL), lambda i: (i, 0))
    return pl.pallas_call(
        _mem_kv_body,
        grid=(n // tm,),
        in_specs=[row, _const_spec((1, D_MODEL)), _const_spec(w_mk.shape), _const_spec(w_mv.shape)],
        out_specs=(row, row),
        out_shape=(jax.ShapeDtypeStruct((n, D_MODEL), F32), jax.ShapeDtypeStruct((n, D_MODEL), F32)),
        compiler_params=_params("arbitrary"),
        name="mem_kv",
    )(mem, g_mem, w_mk, w_mv)


def _xattn_body(q_ref, mk_ref, mv_ref, o_ref):
    scale = XATT_HEAD_DIM ** -0.5
    for gi in range(q_ref.shape[0]):
        for hd in range(N_XATT_HEADS):
            sl = slice(hd * XATT_HEAD_DIM, (hd + 1) * XATT_HEAD_DIM)
            s = _dot_nt(q_ref[gi, :, sl], mk_ref[gi, :, sl].astype(BF16)) * scale
            p = jnp.exp(s - jnp.max(s, axis=-1, keepdims=True))
            den = jnp.sum(p, axis=-1, keepdims=True)
            o = _dot(p.astype(BF16), mv_ref[gi, :, sl].astype(BF16)) / den
            o_ref[gi, :, sl] = o.astype(BF16)


def _xattn(qx, mem_k, mem_v, gb, tm):
    b, l, _ = qx.shape
    qspec = pl.BlockSpec((gb, tm, D_MODEL), lambda bi, j: (bi, j, 0))
    mspec = pl.BlockSpec((gb, N_MEM, D_MODEL), lambda bi, j: (bi, 0, 0))
    return pl.pallas_call(
        _xattn_body,
        grid=(b // gb, l // tm),
        in_specs=[qspec, mspec, mspec],
        out_specs=qspec,
        out_shape=jax.ShapeDtypeStruct((b, l, D_MODEL), BF16),
        compiler_params=_params("arbitrary", "arbitrary"),
        name="xattn",
    )(qx, mem_k, mem_v)


def _post2_body(x1_ref, o_ref, wxo_ref, gmlp_ref, wup_ref, wdown_ref, gfin_ref, y_ref):
    x2 = x1_ref[...] + _dot(o_ref[...], wxo_ref[...])
    hm = _rmsnorm(x2, gmlp_ref[...]).astype(BF16)
    acc = x2
    for c in range(D_FF // D_MODEL):
        sl = slice(c * D_MODEL, (c + 1) * D_MODEL)
        u = jnp.maximum(_dot(hm, wup_ref[:, sl]), 0.0)
        acc = acc + _dot((u * u).astype(BF16), wdown_ref[sl, :])
    y_ref[...] = _rmsnorm(acc, gfin_ref[...])


def _post2(x1, o, w_xo, g_mlp, w_up, w_down, g_final, tm):
    n, _ = x1.shape
    row = pl.BlockSpec((tm, D_MODEL), lambda i: (i, 0))
    return pl.pallas_call(
        _post2_body,
        grid=(n // tm,),
        in_specs=[row, row, _const_spec(w_xo.shape), _const_spec((1, D_MODEL)), _const_spec(w_up.shape),
                  _const_spec(w_down.shape), _const_spec((1, D_MODEL))],
        out_specs=row,
        out_shape=jax.ShapeDtypeStruct((n, D_MODEL), F32),
        compiler_params=_params("arbitrary"),
        name="post2",
    )(x1, o, w_xo, g_mlp, w_up, w_down, g_final)


def _rope_tables(pos):
    half = HEAD_DIM // 2
    inv = ROPE_THETA ** (-jnp.arange(half, dtype=F32) * 2.0 / HEAD_DIM)
    ang = pos.astype(F32)[:, None] * inv[None, :]
    c, s = jnp.cos(ang), jnp.sin(ang)
    return jnp.concatenate([c, c, c, c], axis=1), jnp.concatenate([-s, s, -s, s], axis=1)


def _pad_lanes(t):
    return jnp.pad(t, ((0, 0), (0, LANES - t.shape[1])))


def kernel(x_prompt, x_sample, cache_win_k, cache_win_v, state_conv, state_ssm, cache_mem_k, cache_mem_v,
           mem_prompt, g_mix, w_in, conv_w, conv_b, dt_bias, a_log, d_skip, g_ssm, w_out, g_xatt, g_mem,
           w_xq, w_mk, w_mv, w_xo, g_mlp, w_up, w_down, g_final):
    depth = w_in.shape[0]
    assert depth == 1, "kernel is written for the single-layer trunk of this problem"
    bp, s_len, _ = x_prompt.shape
    bs, t_new, _ = x_sample.shape
    lb = cache_win_k.shape[2]
    lw = min(MAX_WINDOW, s_len)
    n_proj = ATT_WIDTH * 3 + SSM_WIDTH + CONV_CH

    li = 0
    row = lambda t: t[li].reshape(1, -1)
    w_main = w_in[li, :, :n_proj].astype(BF16)
    w_dt = _pad_lanes(w_in[li, :, n_proj:]).astype(BF16)
    conv_w8 = jnp.pad(conv_w[li], ((0, 8 - CONV_WIDTH), (0, 0)))
    dt_b = _pad_lanes(row(dt_bias))
    a_lg = _pad_lanes(row(a_log))
    d_sk = jnp.repeat(d_skip[li], HEAD_DIM).reshape(1, -1)
    w_o, w_q, w_k, w_v, w_x = (t[li].astype(BF16) for t in (w_out, w_xq, w_mk, w_mv, w_xo))
    w_u, w_d = w_up[li].astype(BF16), w_down[li].astype(BF16)
    g_fin = g_final.reshape(1, -1)

    cos_p, sin_p = _rope_tables(jnp.arange(s_len, dtype=jnp.int32))
    q, k, v, k32, v32, z, xbc, dt = _in_proj(x_prompt, row(g_mix), w_main, w_dt, cos_p, sin_p, lw, 256)
    att = _attn_prompt(q, k, v)
    y_ssm, ssm_p = _ssd(xbc, z, dt, jnp.zeros((bp, 8, CONV_CH), F32),
                        jnp.zeros((bp, SSM_WIDTH, SSM_STATE), F32), conv_w8, row(conv_b), dt_b, a_lg, d_sk,
                        row(g_ssm), s_len)
    mk, mv = _mem_kv(mem_prompt.reshape(bp * N_MEM, D_MODEL), row(g_mem), w_k, w_v, 256)
    x1, qx = _post1(x_prompt, att, y_ssm, w_o, row(g_xatt), w_q, 256)
    o = _xattn(qx, mk.reshape(bp, N_MEM, D_MODEL), mv.reshape(bp, N_MEM, D_MODEL), 1, 512)
    y_prompt = _post2(x1.reshape(bp * s_len, D_MODEL), o.reshape(bp * s_len, D_MODEL), w_x, row(g_mlp),
                      w_u, w_d, g_fin, 256)
    y_prompt = y_prompt.reshape(bp, s_len, D_MODEL)
    win_k_p = k32.reshape(1, bp, lw, N_ATT_HEADS, HEAD_DIM)
    win_v_p = v32.reshape(1, bp, lw, N_ATT_HEADS, HEAD_DIM)
    conv_p = xbc[:, s_len - (CONV_WIDTH - 1):][None]
    ssm_p = ssm_p.reshape(1, bp, N_SSM_HEADS, HEAD_DIM, SSM_STATE)
    mk_p = mk.reshape(1, bp, N_MEM, N_XATT_HEADS, XATT_HEAD_DIM)
    mv_p = mv.reshape(1, bp, N_MEM, N_XATT_HEADS, XATT_HEAD_DIM)

    r = SAMPLE_ROWS
    n_s = bs * r
    xs_pad = jnp.pad(x_sample, ((0, 0), (0, r - t_new), (0, 0))).reshape(1, n_s, D_MODEL)
    pos_s = PAST_LEN + (jnp.arange(n_s, dtype=jnp.int32) % r)
    cos_s, sin_s = _rope_tables(pos_s)
    q, _, _, k32, v32, z, xbc, dt = _in_proj(xs_pad, row(g_mix), w_main, w_dt, cos_s, sin_s, n_s, 256)
    to_minor = lambda c: jnp.transpose(c[li].reshape(bs, lb, ATT_WIDTH), (0, 2, 1))
    att, kt_s, vt_s = _attn_sample(q, k32, v32, to_minor(cache_win_k), to_minor(cache_win_v), r, t_new)
    per_seq = lambda t: t.reshape(bs, r, t.shape[-1])
    conv_prev8 = jnp.pad(state_conv[li], ((0, 0), (8 - (CONV_WIDTH - 1), 0), (0, 0)))
    xbc_s = per_seq(xbc)
    y_ssm, ssm_s = _ssd(xbc_s, per_seq(z), per_seq(dt), conv_prev8,
                        state_ssm[li].reshape(bs, SSM_WIDTH, SSM_STATE), conv_w8, row(conv_b), dt_b, a_lg, d_sk,
                        row(g_ssm), t_new)
    x1, qx = _post1(xs_pad, att, y_ssm.reshape(1, n_s, SSM_WIDTH), w_o, row(g_xatt), w_q, 256)
    o = _xattn(qx.reshape(bs, r, D_MODEL), cache_mem_k[li].reshape(bs, N_MEM, D_MODEL),
               cache_mem_v[li].reshape(bs, N_MEM, D_MODEL), 4, r)
    y_s = _post2(x1.reshape(n_s, D_MODEL), o.reshape(n_s, D_MODEL), w_x, row(g_mlp), w_u, w_d, g_fin, 256)
    y_sample = y_s.reshape(bs, r, D_MODEL)[:, :t_new]
    from_minor = lambda c: jnp.transpose(c, (0, 2, 1)).reshape(1, bs, lb, N_ATT_HEADS, HEAD_DIM)
    win_k_s, win_v_s = from_minor(kt_s), from_minor(vt_s)
    conv_s = jnp.concatenate([state_conv[li], xbc_s[:, :t_new]], axis=1)[:, -(CONV_WIDTH - 1):][None]
    ssm_s = ssm_s.reshape(1, bs, N_SSM_HEADS, HEAD_DIM, SSM_STATE)

    return (y_prompt, y_sample, win_k_p, win_v_p, conv_p, ssm_p, mk_p, mv_p,
            win_k_s, win_v_s, conv_s, ssm_s)
```

```python
import functools

import jax
import jax.numpy as jnp
from jax import lax
from jax.experimental import pallas as pl
from jax.experimental.pallas import tpu as pltpu

F32 = jnp.float32
BF16 = jnp.bfloat16

D_MODEL = 1024
PAST_LEN = 8192
HEAD_DIM = 64
ATT_WIDTH = 512
N_ATT_HEADS = 8
DILATED_BRANCHES = ((128, 1), (512, 4), (2048, 16))
MAX_WINDOW = 2048
ROPE_THETA = 10000.0
SSM_WIDTH = 512
N_SSM_HEADS = 8
SSM_GROUPS = 2
SSM_STATE = 128
CONV_WIDTH = 4
SSD_CHUNK = 128
CONV_CH = 1024
N_MEM = 256
N_XATT_HEADS = 4
XATT_HEAD_DIM = 256
D_FF = 4096
EPS = 1e-6

LANES = 128
PAIR = 2 * HEAD_DIM
N_PAIRS = ATT_WIDTH // PAIR
BRANCH_BLOCK = 128
ATT_TILE = 2048
UNITS_PER_ITER = 4
SAMPLE_ROWS = 16
VMEM_LIMIT = 56 * 1024 * 1024


def _params(*sem):
    return pltpu.CompilerParams(dimension_semantics=sem, vmem_limit_bytes=VMEM_LIMIT)


def _const_spec(shape):
    return pl.BlockSpec(shape, lambda *_: (0,) * len(shape), pipeline_mode=pl.Buffered(1))


def _rmsnorm(x, g):
    return x * lax.rsqrt(jnp.mean(x * x, axis=-1, keepdims=True) + EPS) * g


def _dot(a, b):
    return jnp.dot(a, b, preferred_element_type=F32)


def _dot_nt(a, b):
    return lax.dot_general(a, b, (((1,), (1,)), ((), ())), preferred_element_type=F32)


def _dot_tn(a, b):
    return lax.dot_general(a, b, (((0,), (0,)), ((), ())), preferred_element_type=F32)


def _in_proj_body(x_ref, g_ref, w_ref, wdt_ref, cos_ref, sin_ref,
                  q_ref, k_ref, v_ref, k32_ref, v32_ref, z_ref, xbc_ref, dt_ref):
    h = _rmsnorm(x_ref[0], g_ref[...]).astype(BF16)
    tm = h.shape[0]
    cos = jnp.concatenate([cos_ref[...]] * N_PAIRS, axis=1)
    sin = jnp.concatenate([sin_ref[...]] * N_PAIRS, axis=1)
    lane = lax.broadcasted_iota(jnp.int32, (tm, ATT_WIDTH), 1)
    first_half = (lane & (HEAD_DIM // 2)) == 0

    def rope(t):
        partner = jnp.where(first_half,
                            pltpu.roll(t, ATT_WIDTH - HEAD_DIM // 2, axis=1),
                            pltpu.roll(t, HEAD_DIM // 2, axis=1))
        return t * cos + partner * sin

    q = rope(_dot(h, w_ref[:, 0:512]))
    k = rope(_dot(h, w_ref[:, 512:1024]))
    v = _dot(h, w_ref[:, 1024:1536])
    for pr in range(N_PAIRS):
        sl = slice(pr * PAIR, (pr + 1) * PAIR)
        q_ref[0, pr] = q[:, sl]
        k_ref[0, pr] = k[:, sl]
        v_ref[0, pr] = v[:, sl]
    k32_ref[0] = k
    v32_ref[0] = v
    z_ref[0] = _dot(h, w_ref[:, 1536:2048])
    xbc_ref[0] = _dot(h, w_ref[:, 2048:3072])
    dt_ref[0] = _dot(h, wdt_ref[...])


def _in_proj(x, g, w_main, w_dt, cos, sin, window, tm):
    b, l, _ = x.shape
    nj = l // tm
    j0 = (l - window) // tm
    row = lambda bi, j: (bi, j, 0)
    slab = lambda bi, j: (bi, 0, j, 0)
    win = lambda bi, j: (bi, jnp.maximum(j - j0, 0), 0)
    slab_shape = jax.ShapeDtypeStruct((b, N_PAIRS, l, PAIR), F32)
    slab_spec = pl.BlockSpec((1, N_PAIRS, tm, PAIR), slab)
    outs = (
        slab_shape, slab_shape, slab_shape,
        jax.ShapeDtypeStruct((b, window, ATT_WIDTH), F32),
        jax.ShapeDtypeStruct((b, window, ATT_WIDTH), F32),
        jax.ShapeDtypeStruct((b, l, SSM_WIDTH), F32),
        jax.ShapeDtypeStruct((b, l, CONV_CH), F32),
        jax.ShapeDtypeStruct((b, l, LANES), F32),
    )
    return pl.pallas_call(
        _in_proj_body,
        grid=(b, nj),
        in_specs=[
            pl.BlockSpec((1, tm, D_MODEL), row),
            _const_spec((1, D_MODEL)),
            _const_spec(w_main.shape),
            _const_spec(w_dt.shape),
            pl.BlockSpec((tm, LANES), lambda bi, j: (j, 0)),
            pl.BlockSpec((tm, LANES), lambda bi, j: (j, 0)),
        ],
        out_specs=(
            slab_spec, slab_spec, slab_spec,
            pl.BlockSpec((1, tm, ATT_WIDTH), win),
            pl.BlockSpec((1, tm, ATT_WIDTH), win),
            pl.BlockSpec((1, tm, SSM_WIDTH), row),
            pl.BlockSpec((1, tm, CONV_CH), row),
            pl.BlockSpec((1, tm, LANES), row),
        ),
        out_shape=outs,
        compiler_params=_params("arbitrary", "arbitrary"),
        name="in_proj",
    )(x, g, w_main, w_dt, cos, sin)


def _attn_prompt_body(q_ref, kc_ref, kp_ref, vc_ref, vp_ref, o_ref, r0_ref, r1_ref, m0_ref, m1_ref):
    t = pl.program_id(2)
    n = BRANCH_BLOCK
    tile = ATT_TILE

    row = lax.broadcasted_iota(jnp.int32, (2 * n, 2 * n), 0) & (n - 1)
    col = lax.broadcasted_iota(jnp.int32, (2 * n, 2 * n), 1)
    bias_cur = jnp.where(col - n <= row, 0.0, -jnp.inf)
    bias = jnp.where(col < n, jnp.where(col >= row, 0.0, -jnp.inf), bias_cur)
    bias_first = jnp.where(col < n, -jnp.inf, bias_cur)
    low = lax.broadcasted_iota(jnp.int32, (n, PAIR), 1) < HEAD_DIM
    low2 = lax.broadcasted_iota(jnp.int32, (2 * n, PAIR), 1) < HEAD_DIM
    scale = HEAD_DIM ** -0.5

    def rows(start, stride):
        return pl.ds(start, n) if stride == 1 else pl.ds(start, n, stride=stride)

    def prev_block(cur_ref, prev_ref, q0, stride, where):
        if isinstance(where, str):
            if where == "cur":
                return cur_ref[rows(q0 - n * stride, stride), :]
            return prev_ref[rows(tile + q0 - n * stride, stride), :]
        return jnp.where(where, cur_ref[rows(jnp.maximum(q0 - n, 0), 1), :], prev_ref[tile - n:tile, :])

    def run_units(units, stride, merge, last=False):
        loaded = []
        for q0, where, _ in units:
            qv = q_ref[rows(q0, stride), :] * scale
            kk = jnp.concatenate([prev_block(kc_ref, kp_ref, q0, stride, where),
                                  kc_ref[rows(q0, stride), :]], axis=0)
            vv = jnp.concatenate([prev_block(vc_ref, vp_ref, q0, stride, where),
                                  vc_ref[rows(q0, stride), :]], axis=0)
            qm = jnp.concatenate([jnp.where(low, qv, 0.0), jnp.where(low, 0.0, qv)], axis=0).astype(BF16)
            loaded.append((qm, kk.astype(BF16),
                           jnp.where(low2, vv, 1.0).astype(BF16), jnp.where(low2, 1.0, vv).astype(BF16)))
        scores = []
        for (qm, kk, _, _), (_, _, has_prev) in zip(loaded, units):
            b = bias if has_prev is None else jnp.where(has_prev, bias, bias_first)
            scores.append(_dot_nt(qm, kk) + b)
        m_new, alpha = [], []
        for s, (q0, _, _) in zip(scores, units):
            m_row = jnp.max(s, axis=-1, keepdims=True)
            if merge:
                m_old = jnp.concatenate([m0_ref[rows(q0, stride), :], m1_ref[rows(q0, stride), :]], axis=0)
                m = jnp.maximum(m_old, m_row)
                alpha.append(jnp.exp(m_old - m))
            else:
                m = jnp.broadcast_to(m_row, (2 * n, PAIR))
            m_new.append(m)
        probs = [jnp.exp(s - jnp.concatenate([m, m], axis=1)).astype(BF16) for s, m in zip(scores, m_new)]
        for i, (q0, _, _) in enumerate(units):
            _, _, va0, va1 = loaded[i]
            acc0 = _dot(probs[i][0:n], va0)
            acc1 = _dot(probs[i][n:], va1)
            if merge:
                acc0 = acc0 + alpha[i][0:n] * r0_ref[rows(q0, stride), :]
                acc1 = acc1 + alpha[i][n:] * r1_ref[rows(q0, stride), :]
            if last:
                num = jnp.where(low, acc0, acc1)
                den = pltpu.roll(jnp.where(low, acc1, acc0), HEAD_DIM, axis=1)
                o_ref[rows(q0, stride), :] = num / den
            else:
                r0_ref[rows(q0, stride), :] = acc0
                r1_ref[rows(q0, stride), :] = acc1
                m0_ref[rows(q0, stride), :] = m_new[i][0:n]
                m1_ref[rows(q0, stride), :] = m_new[i][n:]

    g_units = UNITS_PER_ITER
    not_first_tile = t > 0

    def branch_d16(it, carry):
        units = [(it * g_units + g, "prev", not_first_tile) for g in range(g_units)]
        run_units(units, 16, merge=False)
        return carry

    lax.fori_loop(0, 16 // g_units, branch_d16, 0)

    def branch_d4(r, carry):
        units = [(i * (4 * n) + r, "prev" if i == 0 else "cur", not_first_tile if i == 0 else None)
                 for i in range(tile // (4 * n))]
        run_units(units, 4, merge=True)
        return carry

    lax.fori_loop(0, 4, branch_d4, 0)

    def branch_d1(it, carry):
        base = pl.multiple_of(it * (g_units * n), g_units * n)
        units = [(base, it > 0, jnp.logical_or(not_first_tile, it > 0))]
        units += [(base + g * n, "cur", None) for g in range(1, g_units)]
        run_units(units, 1, merge=True, last=True)
        return carry

    lax.fori_loop(0, tile // (g_units * n), branch_d1, 0)


def _attn_prompt(q, k, v):
    b, npair, s, w = q.shape
    assert [d for _, d in DILATED_BRANCHES] == [1, 4, 16] and s % ATT_TILE == 0
    tile = ATT_TILE
    cur = pl.BlockSpec((None, None, tile, w), lambda bi, p, t: (bi, p, t, 0))
    prev = pl.BlockSpec((None, None, tile, w), lambda bi, p, t: (bi, p, jnp.maximum(t - 1, 0), 0))
    return pl.pallas_call(
        _attn_prompt_body,
        grid=(b, npair, s // tile),
        in_specs=[cur, cur, prev, cur, prev],
        out_specs=cur,
        out_shape=jax.ShapeDtypeStruct((b, npair, s, w), F32),
        scratch_shapes=[pltpu.VMEM((tile, w), F32)] * 4,
        compiler_params=_params("arbitrary", "arbitrary", "arbitrary"),
        name="attn_prompt",
    )(q, k, k, v, v)


def _branch_count(dist):
    cnt = jnp.zeros(dist.shape, F32)
    for window, dil in DILATED_BRANCHES:
        hit = (dist >= 0) & (dist <= window) & ((dist & (dil - 1)) == 0)
        cnt = cnt + hit.astype(F32)
    return cnt


def _attn_sample_body(q_ref, kn_ref, vn_ref, kt_ref, vt_ref, att_ref, kto_ref, vto_ref, *, n_new):
    lb = kt_ref.shape[1]
    rows = q_ref.shape[1]
    k_new = kn_ref[...]
    v_new = vn_ref[...]

    tail_lane = lax.broadcasted_iota(jnp.int32, (64, LANES), 1) >= LANES - n_new
    for new, src_ref, dst_ref in ((k_new, kt_ref, kto_ref), (v_new, vt_ref, vto_ref)):
        last8 = pltpu.roll(new[0:8], 8 - n_new, axis=0)
        padded = jnp.concatenate([jnp.zeros((LANES - 8, ATT_WIDTH), F32), last8], axis=0)
        new_t = jnp.concatenate([padded[:, p * LANES:(p + 1) * LANES].T for p in range(N_PAIRS)], axis=0)
        for c in range(ATT_WIDTH // 64):
            shifted = pltpu.roll(src_ref[c * 64:(c + 1) * 64, :], lb - n_new, axis=1)
            dst_ref[c * 64:(c + 1) * 64, 0:lb - LANES] = shifted[:, 0:lb - LANES]
            dst_ref[c * 64:(c + 1) * 64, lb - LANES:lb] = jnp.where(
                tail_lane, new_t[c * 64:(c + 1) * 64], shifted[:, lb - LANES:lb])

    t_c = lax.broadcasted_iota(jnp.int32, (rows, lb), 0)
    i_c = lax.broadcasted_iota(jnp.int32, (rows, lb), 1)
    cnt_c = _branch_count(lb + t_c - i_c)
    t_n = lax.broadcasted_iota(jnp.int32, (rows, rows), 0)
    i_n = lax.broadcasted_iota(jnp.int32, (rows, rows), 1)
    cnt_n = jnp.where(i_n < n_new, _branch_count(t_n - i_n), 0.0)
    low = lax.broadcasted_iota(jnp.int32, (rows, PAIR), 1) < HEAD_DIM
    scale = HEAD_DIM ** -0.5
    for pr in range(N_PAIRS):
        sl = slice(pr * PAIR, (pr + 1) * PAIR)
        qp = q_ref[pr]
        ktp = kt_ref[sl, :].astype(BF16)
        vtp = vt_ref[sl, :].astype(BF16)
        knp = k_new[:, sl].astype(BF16)
        vnp = v_new[:, sl].astype(BF16)
        o_h = []
        for hh in range(2):
            qm = jnp.where(low if hh == 0 else jnp.logical_not(low), qp, 0.0).astype(BF16)
            s_c = jnp.where(cnt_c > 0, _dot(qm, ktp) * scale, -jnp.inf)
            s_n = jnp.where(cnt_n > 0, _dot_nt(qm, knp) * scale, -jnp.inf)
            m = jnp.maximum(jnp.max(s_c, axis=-1, keepdims=True), jnp.max(s_n, axis=-1, keepdims=True))
            p_c = cnt_c * jnp.exp(s_c - m)
            p_n = cnt_n * jnp.exp(s_n - m)
            den = jnp.sum(p_c, axis=-1, keepdims=True) + jnp.sum(p_n, axis=-1, keepdims=True)
            pv = _dot_nt(p_c.astype(BF16), vtp) + _dot(p_n.astype(BF16), vnp)
            o_h.append(pv / den)
        att_ref[pr] = jnp.where(low, o_h[0], o_h[1])


def _attn_sample(q, k_new, v_new, kt_cache, vt_cache, rows, n_new):
    b, w, lb = kt_cache.shape
    assert n_new <= 8 and lb % LANES == 0
    slab = pl.BlockSpec((None, N_PAIRS, rows, PAIR), lambda bi: (0, 0, bi, 0))
    new = pl.BlockSpec((None, rows, w), lambda bi: (0, bi, 0))
    cache = pl.BlockSpec((None, w, lb), lambda bi: (bi, 0, 0))
    return pl.pallas_call(
        functools.partial(_attn_sample_body, n_new=n_new),
        grid=(b,),
        in_specs=[slab, new, new, cache, cache],
        out_specs=(slab, cache, cache),
        out_shape=(jax.ShapeDtypeStruct(q.shape, F32),
                   jax.ShapeDtypeStruct((b, w, lb), F32),
                   jax.ShapeDtypeStruct((b, w, lb), F32)),
        compiler_params=_params("arbitrary"),
        name="attn_sample",
    )(q, k_new, v_new, kt_cache, vt_cache)


def _ssd_body(xbc_ref, z_ref, dt_ref, cprev_ref, h0_ref, cw_ref, cb_ref, dtb_ref, alog_ref, dskip_ref, g_ref,
              y_ref, hout_ref, h_ref, xprev_ref, *, rows_in, chunks, n_valid):
    q = SSD_CHUNK

    @pl.when(pl.program_id(1) == 0)
    def _():
        h_ref[...] = h0_ref[0]
        xprev_ref[...] = cprev_ref[0]

    def one_chunk(ck, carry):
        _ssd_chunk(pl.program_id(1) * chunks + ck, pl.multiple_of(ck * q, q) if rows_in >= q else 0,
                   xbc_ref, z_ref, dt_ref, cw_ref, cb_ref, dtb_ref, alog_ref, dskip_ref, g_ref, y_ref,
                   h_ref, xprev_ref, rows_in=rows_in, n_valid=n_valid)
        return carry

    lax.fori_loop(0, chunks, one_chunk, 0)
    hout_ref[0] = h_ref[...]


def _silu(x):
    half = 0.5 * x
    return half + half * jnp.tanh(half)


def _ssd_chunk(chunk, r0, xbc_ref, z_ref, dt_ref, cw_ref, cb_ref, dtb_ref, alog_ref, dskip_ref, g_ref, y_ref,
               h_ref, xprev_ref, *, rows_in, n_valid):
    q = SSD_CHUNK

    def chunk_rows(ref):
        if rows_in >= q:
            return ref[0, pl.ds(r0, q), :]
        t = ref[0]
        return jnp.concatenate([t, jnp.zeros((q - rows_in, t.shape[1]), t.dtype)], axis=0)

    def to_columns(t8):
        return jnp.concatenate([t8, jnp.zeros((LANES - 8, q), F32)], axis=0).T

    cur = chunk_rows(xbc_ref)
    prev8 = xprev_ref[...]
    i8 = lax.broadcasted_iota(jnp.int32, (8, CONV_CH), 0)
    acc = cb_ref[...] + cur * cw_ref[CONV_WIDTH - 1:CONV_WIDTH, :]
    for sh in range(1, CONV_WIDTH):
        rolled = pltpu.roll(cur, sh, axis=0)
        head = jnp.where(i8 < sh, pltpu.roll(prev8, sh, axis=0), rolled[0:8])
        shifted = jnp.concatenate([head, rolled[8:]], axis=0)
        acc = acc + shifted * cw_ref[CONV_WIDTH - 1 - sh:CONV_WIDTH - sh, :]
    xprev_ref[...] = cur[q - 8:q]
    conv = _silu(acc)
    xs = conv[:, 0:SSM_WIDTH]

    tok = lax.broadcasted_iota(jnp.int32, (N_SSM_HEADS, q), 1)
    dt_in = chunk_rows(dt_ref).T[0:N_SSM_HEADS, :] + dtb_ref[...]
    dtv_t = jnp.maximum(dt_in, 0.0) + jnp.log1p(jnp.exp(-jnp.abs(dt_in)))
    dtv_t = jnp.where(chunk * q + tok < n_valid, dtv_t, 0.0)
    a_t = dtv_t * (-jnp.exp(alog_ref[...]))
    row_l = lax.broadcasted_iota(jnp.int32, (q, q), 0)
    lane_l = lax.broadcasted_iota(jnp.int32, (q, q), 1)
    tri = row_l >= lane_l
    a_cs_t = jnp.dot(a_t, (row_l <= lane_l).astype(F32), preferred_element_type=F32,
                     precision=lax.Precision.HIGHEST)
    a_cs = to_columns(a_cs_t)
    dtv = to_columns(dtv_t)
    low = lax.broadcasted_iota(jnp.int32, (q, PAIR), 1) < HEAD_DIM

    def per_pair(col0, col1):
        return jnp.where(low, col0, col1)

    ys = []
    for pr in range(N_SSM_HEADS // 2):
        grp = (2 * pr) // (N_SSM_HEADS // SSM_GROUPS)
        sl = slice(pr * PAIR, (pr + 1) * PAIR)
        bm = conv[:, SSM_WIDTH + grp * SSM_STATE:SSM_WIDTH + (grp + 1) * SSM_STATE]
        cm = conv[:, SSM_WIDTH + (SSM_GROUPS + grp) * SSM_STATE:SSM_WIDTH + (SSM_GROUPS + grp + 1) * SSM_STATE]
        bmb = bm.astype(BF16)
        cmb = cm.astype(BF16)
        cb = _dot_nt(cmb, bmb)
        h0, h1 = 2 * pr, 2 * pr + 1
        cs0, cs1 = a_cs[:, h0:h0 + 1], a_cs[:, h1:h1 + 1]
        tot0, tot1 = a_cs[q - 1:q, h0:h0 + 1], a_cs[q - 1:q, h1:h1 + 1]
        xs_p = xs[:, sl]
        xdt = xs_p * per_pair(dtv[:, h0:h0 + 1], dtv[:, h1:h1 + 1])
        xdt_b = xdt.astype(BF16)
        y_d = []
        for hh, cs in ((h0, cs0), (h1, cs1)):
            decay = jnp.exp(jnp.where(tri, cs - a_cs_t[hh:hh + 1, :], -jnp.inf))
            y_d.append(_dot((cb * decay).astype(BF16), xdt_b))
        y_diag = per_pair(y_d[0], y_d[1])
        to_end = per_pair(jnp.exp(tot0 - cs0), jnp.exp(tot1 - cs1))
        states = _dot_tn((xdt * to_end).astype(BF16), bmb)
        h_pair = h_ref[pr * PAIR:(pr + 1) * PAIR, :]
        y_off = _dot_nt(cmb, h_pair.astype(BF16)) * per_pair(jnp.exp(cs0), jnp.exp(cs1))
        chunk_decay = jnp.concatenate([jnp.broadcast_to(jnp.exp(tot0), (HEAD_DIM, SSM_STATE)),
                                       jnp.broadcast_to(jnp.exp(tot1), (HEAD_DIM, SSM_STATE))], axis=0)
        h_ref[pr * PAIR:(pr + 1) * PAIR, :] = h_pair * chunk_decay + states
        ys.append(y_diag + y_off + dskip_ref[:, sl] * xs_p)
    y = jnp.concatenate(ys, axis=1)

    u = y * _silu(chunk_rows(z_ref))
    gw = SSM_WIDTH // SSM_GROUPS
    normed = []
    for grp in range(SSM_GROUPS):
        ug = u[:, grp * gw:(grp + 1) * gw]
        normed.append(ug * lax.rsqrt(jnp.mean(ug * ug, axis=-1, keepdims=True) + EPS))
    out = jnp.concatenate(normed, axis=1) * g_ref[...]
    if rows_in >= q:
        y_ref[0, pl.ds(r0, q), :] = out
    else:
        y_ref[0] = out[0:rows_in]


def _ssd(xbc, z, dt, conv_prev8, h0, conv_w8, conv_b, dt_bias, a_log, d_skip, g_ssm, n_valid):
    b, l, _ = xbc.shape
    chunks = 4 if l % (4 * SSD_CHUNK) == 0 else 1
    rows_in = min(SSD_CHUNK * chunks, l)
    nc = l // rows_in
    row = lambda bi, c: (bi, c, 0)
    per_b = lambda bi, c: (bi, 0, 0)
    return pl.pallas_call(
        functools.partial(_ssd_body, rows_in=rows_in, chunks=chunks, n_valid=n_valid),
        grid=(b, nc),
        in_specs=[
            pl.BlockSpec((1, rows_in, CONV_CH), row),
            pl.BlockSpec((1, rows_in, SSM_WIDTH), row),
            pl.BlockSpec((1, rows_in, LANES), row),
            pl.BlockSpec((1, 8, CONV_CH), per_b),
            pl.BlockSpec((1, SSM_WIDTH, SSM_STATE), per_b),
            _const_spec((8, CONV_CH)),
            _const_spec((1, CONV_CH)),
            _const_spec((N_SSM_HEADS, SSD_CHUNK)),
            _const_spec((N_SSM_HEADS, SSD_CHUNK)),
            _const_spec((1, SSM_WIDTH)),
            _const_spec((1, SSM_WIDTH)),
        ],
        out_specs=(pl.BlockSpec((1, rows_in, SSM_WIDTH), row),
                   pl.BlockSpec((1, SSM_WIDTH, SSM_STATE), per_b)),
        out_shape=(jax.ShapeDtypeStruct((b, l, SSM_WIDTH), F32),
                   jax.ShapeDtypeStruct((b, SSM_WIDTH, SSM_STATE), F32)),
        scratch_shapes=[pltpu.VMEM((SSM_WIDTH, SSM_STATE), F32), pltpu.VMEM((8, CONV_CH), F32)],
        compiler_params=_params("arbitrary", "arbitrary"),
        name="ssd",
    )(xbc, z, dt, conv_prev8, h0, conv_w8, conv_b, dt_bias, a_log, d_skip, g_ssm)


def _post1_body(x_ref, att_ref, y_ref, wout_ref, g_ref, wxq_ref, x1_ref, qx_ref):
    mixed = jnp.concatenate([att_ref[0, pr] for pr in range(N_PAIRS)] + [y_ref[0]], axis=1).astype(BF16)
    x1 = x_ref[0] + _dot(mixed, wout_ref[...])
    x1_ref[0] = x1
    qx_ref[0] = _dot(_rmsnorm(x1, g_ref[...]).astype(BF16), wxq_ref[...]).astype(BF16)


def _post1(x, att, y, w_out, g_xatt, w_xq, tm):
    b, l, _ = x.shape
    row = lambda bi, j: (bi, j, 0)
    full = pl.BlockSpec((1, tm, D_MODEL), row)
    return pl.pallas_call(
        _post1_body,
        grid=(b, l // tm),
        in_specs=[full,
                  pl.BlockSpec((1, N_PAIRS, tm, PAIR), lambda bi, j: (bi, 0, j, 0)),
                  pl.BlockSpec((1, tm, SSM_WIDTH), row),
                  _const_spec(w_out.shape), _const_spec((1, D_MODEL)), _const_spec(w_xq.shape)],
        out_specs=(full, full),
        out_shape=(jax.ShapeDtypeStruct((b, l, D_MODEL), F32), jax.ShapeDtypeStruct((b, l, D_MODEL), BF16)),
        compiler_params=_params("arbitrary", "arbitrary"),
        name="post1",
    )(x, att, y, w_out, g_xatt, w_xq)


def _mem_kv_body(m_ref, g_ref, wk_ref, wv_ref, k_ref, v_ref):
    h = _rmsnorm(m_ref[...], g_ref[...]).astype(BF16)
    k_ref[...] = _dot(h, wk_ref[...])
    v_ref[...] = _dot(h, wv_ref[...])


def _mem_kv(mem, g_mem, w_mk, w_mv, tm):
    n, _ = mem.shape
    row = pl.BlockSpec((tm, D_MODEL), lambda i: (i, 0))
    return pl.pallas_call(
        _mem_kv_body,
        grid=(n // tm,),
        in_specs=[row, _const_spec((1, D_MODEL)), _const_spec(w_mk.shape), _const_spec(w_mv.shape)],
        out_specs=(row, row),
        out_shape=(jax.ShapeDtypeStruct((n, D_MODEL), F32), jax.ShapeDtypeStruct((n, D_MODEL), F32)),
        compiler_params=_params("arbitrary"),
        name="mem_kv",
    )(mem, g_mem, w_mk, w_mv)


def _xattn_body(q_ref, mk_ref, mv_ref, o_ref):
    scale = XATT_HEAD_DIM ** -0.5
    for gi in range(q_ref.shape[0]):
        for hd in range(N_XATT_HEADS):
            sl = slice(hd * XATT_HEAD_DIM, (hd + 1) * XATT_HEAD_DIM)
            s = _dot_nt(q_ref[gi, :, sl], mk_ref[gi, :, sl].astype(BF16)) * scale
            p = jnp.exp(s - jnp.max(s, axis=-1, keepdims=True))
            den = jnp.sum(p, axis=-1, keepdims=True)
            o = _dot(p.astype(BF16), mv_ref[gi, :, sl].astype(BF16)) / den
            o_ref[gi, :, sl] = o.astype(BF16)


def _xattn(qx, mem_k, mem_v, gb, tm):
    b, l, _ = qx.shape
    qspec = pl.BlockSpec((gb, tm, D_MODEL), lambda bi, j: (bi, j, 0))
    mspec = pl.BlockSpec((gb, N_MEM, D_MODEL), lambda bi, j: (bi, 0, 0))
    return pl.pallas_call(
        _xattn_body,
        grid=(b // gb, l // tm),
        in_specs=[qspec, mspec, mspec],
        out_specs=qspec,
        out_shape=jax.ShapeDtypeStruct((b, l, D_MODEL), BF16),
        compiler_params=_params("arbitrary", "arbitrary"),
        name="xattn",
    )(qx, mem_k, mem_v)


def _post2_body(x1_ref, o_ref, wxo_ref, gmlp_ref, wup_ref, wdown_ref, gfin_ref, y_ref):
    x2 = x1_ref[...] + _dot(o_ref[...], wxo_ref[...])
    hm = _rmsnorm(x2, gmlp_ref[...]).astype(BF16)
    acc = x2
    for c in range(D_FF // D_MODEL):
        sl = slice(c * D_MODEL, (c + 1) * D_MODEL)
        u = jnp.maximum(_dot(hm, wup_ref[:, sl]), 0.0)
        acc = acc + _dot((u * u).astype(BF16), wdown_ref[sl, :])
    y_ref[...] = _rmsnorm(acc, gfin_ref[...])


def _post2(x1, o, w_xo, g_mlp, w_up, w_down, g_final, tm):
    n, _ = x1.shape
    row = pl.BlockSpec((tm, D_MODEL), lambda i: (i, 0))
    return pl.pallas_call(
        _post2_body,
        grid=(n // tm,),
        in_specs=[row, row, _const_spec(w_xo.shape), _const_spec((1, D_MODEL)), _const_spec(w_up.shape),
                  _const_spec(w_down.shape), _const_spec((1, D_MODEL))],
        out_specs=row,
        out_shape=jax.ShapeDtypeStruct((n, D_MODEL), F32),
        compiler_params=_params("arbitrary"),
        name="post2",
    )(x1, o, w_xo, g_mlp, w_up, w_down, g_final)


def _rope_tables(pos):
    half = HEAD_DIM // 2
    inv = ROPE_THETA ** (-jnp.arange(half, dtype=F32) * 2.0 / HEAD_DIM)
    ang = pos.astype(F32)[:, None] * inv[None, :]
    c, s = jnp.cos(ang), jnp.sin(ang)
    return jnp.concatenate([c, c, c, c], axis=1), jnp.concatenate([-s, s, -s, s], axis=1)


def _pad_lanes(t):
    return jnp.pad(t, ((0, 0), (0, LANES - t.shape[1])))


def kernel(x_prompt, x_sample, cache_win_k, cache_win_v, state_conv, state_ssm, cache_mem_k, cache_mem_v,
           mem_prompt, g_mix, w_in, conv_w, conv_b, dt_bias, a_log, d_skip, g_ssm, w_out, g_xatt, g_mem,
           w_xq, w_mk, w_mv, w_xo, g_mlp, w_up, w_down, g_final):
    depth = w_in.shape[0]
    assert depth == 1, "kernel is written for the single-layer trunk of this problem"
    bp, s_len, _ = x_prompt.shape
    bs, t_new, _ = x_sample.shape
    lb = cache_win_k.shape[2]
    lw = min(MAX_WINDOW, s_len)
    n_proj = ATT_WIDTH * 3 + SSM_WIDTH + CONV_CH

    li = 0
    row = lambda t: t[li].reshape(1, -1)
    w_main = w_in[li, :, :n_proj].astype(BF16)
    w_dt = _pad_lanes(w_in[li, :, n_proj:]).astype(BF16)
    conv_w8 = jnp.pad(conv_w[li], ((0, 8 - CONV_WIDTH), (0, 0)))
    per_head_rows = lambda t: jnp.broadcast_to(t[li][:, None], (N_SSM_HEADS, SSD_CHUNK))
    dt_b = per_head_rows(dt_bias)
    a_lg = per_head_rows(a_log)
    d_sk = jnp.repeat(d_skip[li], HEAD_DIM).reshape(1, -1)
    w_o, w_q, w_k, w_v, w_x = (t[li].astype(BF16) for t in (w_out, w_xq, w_mk, w_mv, w_xo))
    w_u, w_d = w_up[li].astype(BF16), w_down[li].astype(BF16)
    g_fin = g_final.reshape(1, -1)

    cos_p, sin_p = _rope_tables(jnp.arange(s_len, dtype=jnp.int32))
    q, k, v, k32, v32, z, xbc, dt = _in_proj(x_prompt, row(g_mix), w_main, w_dt, cos_p, sin_p, lw, 256)
    att = _attn_prompt(q, k, v)
    y_ssm, ssm_p = _ssd(xbc, z, dt, jnp.zeros((bp, 8, CONV_CH), F32),
                        jnp.zeros((bp, SSM_WIDTH, SSM_STATE), F32), conv_w8, row(conv_b), dt_b, a_lg, d_sk,
                        row(g_ssm), s_len)
    mk, mv = _mem_kv(mem_prompt.reshape(bp * N_MEM, D_MODEL), row(g_mem), w_k, w_v, 256)
    x1, qx = _post1(x_prompt, att, y_ssm, w_o, row(g_xatt), w_q, 256)
    o = _xattn(qx, mk.reshape(bp, N_MEM, D_MODEL), mv.reshape(bp, N_MEM, D_MODEL), 1, 512)
    y_prompt = _post2(x1.reshape(bp * s_len, D_MODEL), o.reshape(bp * s_len, D_MODEL), w_x, row(g_mlp),
                      w_u, w_d, g_fin, 256)
    y_prompt = y_prompt.reshape(bp, s_len, D_MODEL)
    win_k_p = k32.reshape(1, bp, lw, N_ATT_HEADS, HEAD_DIM)
    win_v_p = v32.reshape(1, bp, lw, N_ATT_HEADS, HEAD_DIM)
    conv_p = xbc[:, s_len - (CONV_WIDTH - 1):][None]
    ssm_p = ssm_p.reshape(1, bp, N_SSM_HEADS, HEAD_DIM, SSM_STATE)
    mk_p = mk.reshape(1, bp, N_MEM, N_XATT_HEADS, XATT_HEAD_DIM)
    mv_p = mv.reshape(1, bp, N_MEM, N_XATT_HEADS, XATT_HEAD_DIM)

    r = SAMPLE_ROWS
    n_s = bs * r
    xs_pad = jnp.pad(x_sample, ((0, 0), (0, r - t_new), (0, 0))).reshape(1, n_s, D_MODEL)
    pos_s = PAST_LEN + (jnp.arange(n_s, dtype=jnp.int32) % r)
    cos_s, sin_s = _rope_tables(pos_s)
    q, _, _, k32, v32, z, xbc, dt = _in_proj(xs_pad, row(g_mix), w_main, w_dt, cos_s, sin_s, n_s, 256)
    to_minor = lambda c: jnp.transpose(c[li].reshape(bs, lb, ATT_WIDTH), (0, 2, 1))
    att, kt_s, vt_s = _attn_sample(q, k32, v32, to_minor(cache_win_k), to_minor(cache_win_v), r, t_new)
    per_seq = lambda t: t.reshape(bs, r, t.shape[-1])
    conv_prev8 = jnp.pad(state_conv[li], ((0, 0), (8 - (CONV_WIDTH - 1), 0), (0, 0)))
    xbc_s = per_seq(xbc)
    y_ssm, ssm_s = _ssd(xbc_s, per_seq(z), per_seq(dt), conv_prev8,
                        state_ssm[li].reshape(bs, SSM_WIDTH, SSM_STATE), conv_w8, row(conv_b), dt_b, a_lg, d_sk,
                        row(g_ssm), t_new)
    x1, qx = _post1(xs_pad, att, y_ssm.reshape(1, n_s, SSM_WIDTH), w_o, row(g_xatt), w_q, 256)
    o = _xattn(qx.reshape(bs, r, D_MODEL), cache_mem_k[li].reshape(bs, N_MEM, D_MODEL),
               cache_mem_v[li].reshape(bs, N_MEM, D_MODEL), 4, r)
    y_s = _post2(x1.reshape(n_s, D_MODEL), o.reshape(n_s, D_MODEL), w_x, row(g_mlp), w_u, w_d, g_fin, 256)
    y_sample = y_s.reshape(bs, r, D_MODEL)[:, :t_new]
    from_minor = lambda c: jnp.transpose(c, (0, 2, 1)).reshape(1, bs, lb, N_ATT_HEADS, HEAD_DIM)
    win_k_s, win_v_s = from_minor(kt_s), from_minor(vt_s)
    conv_s = jnp.concatenate([state_conv[li], xbc_s[:, :t_new]], axis=1)[:, -(CONV_WIDTH - 1):][None]
    ssm_s = ssm_s.reshape(1, bs, N_SSM_HEADS, HEAD_DIM, SSM_STATE)

    return (y_prompt, y_sample, win_k_p, win_v_p, conv_p, ssm_p, mk_p, mv_p,
            win_k_s, win_v_s, conv_s, ssm_s)
```

```python
import functools

import jax
import jax.numpy as jnp
from jax import lax
from jax.experimental import pallas as pl
from jax.experimental.pallas import tpu as pltpu

F32 = jnp.float32
BF16 = jnp.bfloat16

D_MODEL = 1024
PAST_LEN = 8192
HEAD_DIM = 64
ATT_WIDTH = 512
N_ATT_HEADS = 8
DILATED_BRANCHES = ((128, 1), (512, 4), (2048, 16))
MAX_WINDOW = 2048
ROPE_THETA = 10000.0
SSM_WIDTH = 512
N_SSM_HEADS = 8
SSM_GROUPS = 2
SSM_STATE = 128
CONV_WIDTH = 4
SSD_CHUNK = 128
CONV_CH = 1024
N_MEM = 256
N_XATT_HEADS = 4
XATT_HEAD_DIM = 256
D_FF = 4096
EPS = 1e-6

LANES = 128
PAIR = 2 * HEAD_DIM
N_PAIRS = ATT_WIDTH // PAIR
BRANCH_BLOCK = 128
ATT_TILE = 2048
UNITS_PER_ITER = 4
ROW_TILE = 512
SAMPLE_ROWS = 16
VMEM_LIMIT = 56 * 1024 * 1024


def _params(*sem):
    return pltpu.CompilerParams(dimension_semantics=sem, vmem_limit_bytes=VMEM_LIMIT)


def _const_spec(shape):
    return pl.BlockSpec(shape, lambda *_: (0,) * len(shape), pipeline_mode=pl.Buffered(1))


def _rmsnorm(x, g):
    return x * lax.rsqrt(jnp.mean(x * x, axis=-1, keepdims=True) + EPS) * g


def _dot(a, b):
    return jnp.dot(a, b, preferred_element_type=F32)


def _dot_nt(a, b):
    return lax.dot_general(a, b, (((1,), (1,)), ((), ())), preferred_element_type=F32)


def _dot_tn(a, b):
    return lax.dot_general(a, b, (((0,), (0,)), ((), ())), preferred_element_type=F32)


def _in_proj_body(x_ref, g_ref, w_ref, wdt_ref, cos_ref, sin_ref,
                  q_ref, k_ref, v_ref, k32_ref, v32_ref, z_ref, xbc_ref, dt_ref):
    h = _rmsnorm(x_ref[0], g_ref[...]).astype(BF16)
    tm = h.shape[0]
    cos = jnp.concatenate([cos_ref[...]] * N_PAIRS, axis=1)
    sin = jnp.concatenate([sin_ref[...]] * N_PAIRS, axis=1)
    lane = lax.broadcasted_iota(jnp.int32, (tm, ATT_WIDTH), 1)
    first_half = (lane & (HEAD_DIM // 2)) == 0

    def rope(t):
        partner = jnp.where(first_half,
                            pltpu.roll(t, ATT_WIDTH - HEAD_DIM // 2, axis=1),
                            pltpu.roll(t, HEAD_DIM // 2, axis=1))
        return t * cos + partner * sin

    q = rope(_dot(h, w_ref[:, 0:512]))
    k = rope(_dot(h, w_ref[:, 512:1024]))
    v = _dot(h, w_ref[:, 1024:1536])
    for pr in range(N_PAIRS):
        sl = slice(pr * PAIR, (pr + 1) * PAIR)
        q_ref[0, pr] = q[:, sl]
        k_ref[0, pr] = k[:, sl]
        v_ref[0, pr] = v[:, sl]
    k32_ref[0] = k
    v32_ref[0] = v
    z_ref[0] = _dot(h, w_ref[:, 1536:2048])
    xbc_ref[0] = _dot(h, w_ref[:, 2048:3072])
    dt_ref[0] = _dot(h, wdt_ref[...])


def _in_proj(x, g, w_main, w_dt, cos, sin, window, tm):
    b, l, _ = x.shape
    nj = l // tm
    j0 = (l - window) // tm
    row = lambda bi, j: (bi, j, 0)
    slab = lambda bi, j: (bi, 0, j, 0)
    win = lambda bi, j: (bi, jnp.maximum(j - j0, 0), 0)
    slab_shape = jax.ShapeDtypeStruct((b, N_PAIRS, l, PAIR), F32)
    slab_spec = pl.BlockSpec((1, N_PAIRS, tm, PAIR), slab)
    outs = (
        slab_shape, slab_shape, slab_shape,
        jax.ShapeDtypeStruct((b, window, ATT_WIDTH), F32),
        jax.ShapeDtypeStruct((b, window, ATT_WIDTH), F32),
        jax.ShapeDtypeStruct((b, l, SSM_WIDTH), F32),
        jax.ShapeDtypeStruct((b, l, CONV_CH), F32),
        jax.ShapeDtypeStruct((b, l, LANES), F32),
    )
    return pl.pallas_call(
        _in_proj_body,
        grid=(b, nj),
        in_specs=[
            pl.BlockSpec((1, tm, D_MODEL), row),
            _const_spec((1, D_MODEL)),
            _const_spec(w_main.shape),
            _const_spec(w_dt.shape),
            pl.BlockSpec((tm, LANES), lambda bi, j: (j, 0)),
            pl.BlockSpec((tm, LANES), lambda bi, j: (j, 0)),
        ],
        out_specs=(
            slab_spec, slab_spec, slab_spec,
            pl.BlockSpec((1, tm, ATT_WIDTH), win),
            pl.BlockSpec((1, tm, ATT_WIDTH), win),
            pl.BlockSpec((1, tm, SSM_WIDTH), row),
            pl.BlockSpec((1, tm, CONV_CH), row),
            pl.BlockSpec((1, tm, LANES), row),
        ),
        out_shape=outs,
        compiler_params=_params("arbitrary", "arbitrary"),
        name="in_proj",
    )(x, g, w_main, w_dt, cos, sin)


def _attn_prompt_body(q_ref, kc_ref, kp_ref, vc_ref, vp_ref, o_ref, r0_ref, r1_ref, m0_ref, m1_ref):
    t = pl.program_id(2)
    n = BRANCH_BLOCK
    tile = ATT_TILE

    row = lax.broadcasted_iota(jnp.int32, (2 * n, 2 * n), 0) & (n - 1)
    col = lax.broadcasted_iota(jnp.int32, (2 * n, 2 * n), 1)
    bias_cur = jnp.where(col - n <= row, 0.0, -jnp.inf)
    bias = jnp.where(col < n, jnp.where(col >= row, 0.0, -jnp.inf), bias_cur)
    bias_first = jnp.where(col < n, -jnp.inf, bias_cur)
    low = lax.broadcasted_iota(jnp.int32, (n, PAIR), 1) < HEAD_DIM
    low2 = lax.broadcasted_iota(jnp.int32, (2 * n, PAIR), 1) < HEAD_DIM
    scale = HEAD_DIM ** -0.5

    def rows(start, stride):
        return pl.ds(start, n) if stride == 1 else pl.ds(start, n, stride=stride)

    def prev_block(cur_ref, prev_ref, q0, stride, where):
        if isinstance(where, str):
            if where == "cur":
                return cur_ref[rows(q0 - n * stride, stride), :]
            return prev_ref[rows(tile + q0 - n * stride, stride), :]
        return jnp.where(where, cur_ref[rows(jnp.maximum(q0 - n, 0), 1), :], prev_ref[tile - n:tile, :])

    def run_units(units, stride, merge, last=False):
        loaded = []
        for q0, where, _ in units:
            qv = q_ref[rows(q0, stride), :] * scale
            kk = jnp.concatenate([prev_block(kc_ref, kp_ref, q0, stride, where),
                                  kc_ref[rows(q0, stride), :]], axis=0)
            vv = jnp.concatenate([prev_block(vc_ref, vp_ref, q0, stride, where),
                                  vc_ref[rows(q0, stride), :]], axis=0)
            qm = jnp.concatenate([jnp.where(low, qv, 0.0), jnp.where(low, 0.0, qv)], axis=0).astype(BF16)
            loaded.append((qm, kk.astype(BF16),
                           jnp.where(low2, vv, 1.0).astype(BF16), jnp.where(low2, 1.0, vv).astype(BF16)))
        scores = []
        for (qm, kk, _, _), (_, _, has_prev) in zip(loaded, units):
            b = bias if has_prev is None else jnp.where(has_prev, bias, bias_first)
            scores.append(_dot_nt(qm, kk) + b)
        m_new, alpha = [], []
        for s, (q0, _, _) in zip(scores, units):
            m_row = jnp.max(s, axis=-1, keepdims=True)
            if merge:
                m_old = jnp.concatenate([m0_ref[rows(q0, stride), :], m1_ref[rows(q0, stride), :]], axis=0)
                m = jnp.maximum(m_old, m_row)
                alpha.append(jnp.exp(m_old - m))
            else:
                m = jnp.broadcast_to(m_row, (2 * n, PAIR))
            m_new.append(m)
        probs = [jnp.exp(s - jnp.concatenate([m, m], axis=1)).astype(BF16) for s, m in zip(scores, m_new)]
        for i, (q0, _, _) in enumerate(units):
            _, _, va0, va1 = loaded[i]
            acc0 = _dot(probs[i][0:n], va0)
            acc1 = _dot(probs[i][n:], va1)
            if merge:
                acc0 = acc0 + alpha[i][0:n] * r0_ref[rows(q0, stride), :]
                acc1 = acc1 + alpha[i][n:] * r1_ref[rows(q0, stride), :]
            if last:
                num = jnp.where(low, acc0, acc1)
                den = pltpu.roll(jnp.where(low, acc1, acc0), HEAD_DIM, axis=1)
                o_ref[rows(q0, stride), :] = num / den
            else:
                r0_ref[rows(q0, stride), :] = acc0
                r1_ref[rows(q0, stride), :] = acc1
                m0_ref[rows(q0, stride), :] = m_new[i][0:n]
                m1_ref[rows(q0, stride), :] = m_new[i][n:]

    g_units = UNITS_PER_ITER
    not_first_tile = t > 0

    def branch_d16(it, carry):
        units = [(it * g_units + g, "prev", not_first_tile) for g in range(g_units)]
        run_units(units, 16, merge=False)
        return carry

    lax.fori_loop(0, 16 // g_units, branch_d16, 0)

    def branch_d4(r, carry):
        units = [(i * (4 * n) + r, "prev" if i == 0 else "cur", not_first_tile if i == 0 else None)
                 for i in range(tile // (4 * n))]
        run_units(units, 4, merge=True)
        return carry

    lax.fori_loop(0, 4, branch_d4, 0)

    def branch_d1(it, carry):
        base = pl.multiple_of(it * (g_units * n), g_units * n)
        units = [(base, it > 0, jnp.logical_or(not_first_tile, it > 0))]
        units += [(base + g * n, "cur", None) for g in range(1, g_units)]
        run_units(units, 1, merge=True, last=True)
        return carry

    lax.fori_loop(0, tile // (g_units * n), branch_d1, 0)


def _attn_prompt(q, k, v):
    b, npair, s, w = q.shape
    assert [d for _, d in DILATED_BRANCHES] == [1, 4, 16] and s % ATT_TILE == 0
    tile = ATT_TILE
    cur = pl.BlockSpec((None, None, tile, w), lambda bi, p, t: (bi, p, t, 0))
    prev = pl.BlockSpec((None, None, tile, w), lambda bi, p, t: (bi, p, jnp.maximum(t - 1, 0), 0))
    return pl.pallas_call(
        _attn_prompt_body,
        grid=(b, npair, s // tile),
        in_specs=[cur, cur, prev, cur, prev],
        out_specs=cur,
        out_shape=jax.ShapeDtypeStruct((b, npair, s, w), F32),
        scratch_shapes=[pltpu.VMEM((tile, w), F32)] * 4,
        compiler_params=_params("arbitrary", "arbitrary", "arbitrary"),
        name="attn_prompt",
    )(q, k, k, v, v)


def _branch_count(dist):
    cnt = jnp.zeros(dist.shape, F32)
    for window, dil in DILATED_BRANCHES:
        hit = (dist >= 0) & (dist <= window) & ((dist & (dil - 1)) == 0)
        cnt = cnt + hit.astype(F32)
    return cnt


def _attn_sample_body(q_ref, kn_ref, vn_ref, kt_ref, vt_ref, att_ref, kto_ref, vto_ref, *, n_new):
    lb = kt_ref.shape[1]
    rows = q_ref.shape[1]
    k_new = kn_ref[...]
    v_new = vn_ref[...]

    tail_lane = lax.broadcasted_iota(jnp.int32, (64, LANES), 1) >= LANES - n_new
    for new, src_ref, dst_ref in ((k_new, kt_ref, kto_ref), (v_new, vt_ref, vto_ref)):
        last8 = pltpu.roll(new[0:8], 8 - n_new, axis=0)
        padded = jnp.concatenate([jnp.zeros((LANES - 8, ATT_WIDTH), F32), last8], axis=0)
        new_t = jnp.concatenate([padded[:, p * LANES:(p + 1) * LANES].T for p in range(N_PAIRS)], axis=0)
        for c in range(ATT_WIDTH // 64):
            shifted = pltpu.roll(src_ref[c * 64:(c + 1) * 64, :], lb - n_new, axis=1)
            dst_ref[c * 64:(c + 1) * 64, 0:lb - LANES] = shifted[:, 0:lb - LANES]
            dst_ref[c * 64:(c + 1) * 64, lb - LANES:lb] = jnp.where(
                tail_lane, new_t[c * 64:(c + 1) * 64], shifted[:, lb - LANES:lb])

    t_c = lax.broadcasted_iota(jnp.int32, (rows, lb), 0)
    i_c = lax.broadcasted_iota(jnp.int32, (rows, lb), 1)
    cnt_c = _branch_count(lb + t_c - i_c)
    t_n = lax.broadcasted_iota(jnp.int32, (rows, rows), 0)
    i_n = lax.broadcasted_iota(jnp.int32, (rows, rows), 1)
    cnt_n = jnp.where(i_n < n_new, _branch_count(t_n - i_n), 0.0)
    low = lax.broadcasted_iota(jnp.int32, (rows, PAIR), 1) < HEAD_DIM
    scale = HEAD_DIM ** -0.5
    for pr in range(N_PAIRS):
        sl = slice(pr * PAIR, (pr + 1) * PAIR)
        qp = q_ref[pr]
        ktp = kt_ref[sl, :].astype(BF16)
        vtp = vt_ref[sl, :].astype(BF16)
        knp = k_new[:, sl].astype(BF16)
        vnp = v_new[:, sl].astype(BF16)
        o_h = []
        for hh in range(2):
            qm = jnp.where(low if hh == 0 else jnp.logical_not(low), qp, 0.0).astype(BF16)
            s_c = jnp.where(cnt_c > 0, _dot(qm, ktp) * scale, -jnp.inf)
            s_n = jnp.where(cnt_n > 0, _dot_nt(qm, knp) * scale, -jnp.inf)
            m = jnp.maximum(jnp.max(s_c, axis=-1, keepdims=True), jnp.max(s_n, axis=-1, keepdims=True))
            p_c = cnt_c * jnp.exp(s_c - m)
            p_n = cnt_n * jnp.exp(s_n - m)
            den = jnp.sum(p_c, axis=-1, keepdims=True) + jnp.sum(p_n, axis=-1, keepdims=True)
            pv = _dot_nt(p_c.astype(BF16), vtp) + _dot(p_n.astype(BF16), vnp)
            o_h.append(pv / den)
        att_ref[pr] = jnp.where(low, o_h[0], o_h[1])


def _attn_sample(q, k_new, v_new, kt_cache, vt_cache, rows, n_new):
    b, w, lb = kt_cache.shape
    assert n_new <= 8 and lb % LANES == 0
    slab = pl.BlockSpec((None, N_PAIRS, rows, PAIR), lambda bi: (0, 0, bi, 0))
    new = pl.BlockSpec((None, rows, w), lambda bi: (0, bi, 0))
    cache = pl.BlockSpec((None, w, lb), lambda bi: (bi, 0, 0))
    return pl.pallas_call(
        functools.partial(_attn_sample_body, n_new=n_new),
        grid=(b,),
        in_specs=[slab, new, new, cache, cache],
        out_specs=(slab, cache, cache),
        out_shape=(jax.ShapeDtypeStruct(q.shape, F32),
                   jax.ShapeDtypeStruct((b, w, lb), F32),
                   jax.ShapeDtypeStruct((b, w, lb), F32)),
        compiler_params=_params("arbitrary"),
        name="attn_sample",
    )(q, k_new, v_new, kt_cache, vt_cache)


def _ssd_body(xbc_ref, z_ref, dt_ref, cprev_ref, h0_ref, cw_ref, cb_ref, dtb_ref, alog_ref, dskip_ref, g_ref,
              y_ref, hout_ref, h_ref, xprev_ref, *, rows_in, chunks, n_valid):
    q = SSD_CHUNK

    @pl.when(pl.program_id(1) == 0)
    def _():
        h_ref[...] = h0_ref[0]
        for sb in range(CONV_CH // LANES):
            xprev_ref[sb, pl.ds(0, 8, stride=2), :] = cprev_ref[0, :, sb * LANES:(sb + 1) * LANES]

    def one_chunk(ck, carry):
        _ssd_chunk(pl.program_id(1) * chunks + ck, pl.multiple_of(ck * q, q) if rows_in >= q else 0,
                   xbc_ref, z_ref, dt_ref, cw_ref, cb_ref, dtb_ref, alog_ref, dskip_ref, g_ref, y_ref,
                   h_ref, xprev_ref, rows_in=rows_in, n_valid=n_valid)
        return carry

    lax.fori_loop(0, chunks, one_chunk, 0)
    hout_ref[0] = h_ref[...]


def _silu(x):
    half = 0.5 * x
    return half + half * jnp.tanh(half)


def _ssd_chunk(chunk, r0, xbc_ref, z_ref, dt_ref, cw_ref, cb_ref, dtb_ref, alog_ref, dskip_ref, g_ref, y_ref,
               h_ref, xprev_ref, *, rows_in, n_valid):
    q = SSD_CHUNK

    def chunk_rows(ref):
        if rows_in >= q:
            return ref[0, pl.ds(r0, q), :]
        t = ref[0]
        return jnp.concatenate([t, jnp.zeros((q - rows_in, t.shape[1]), t.dtype)], axis=0)

    def to_columns(t8):
        return jnp.concatenate([t8, jnp.zeros((LANES - 8, q), F32)], axis=0).T

    cur = chunk_rows(xbc_ref)
    slabs = []
    for sb in range(CONV_CH // LANES):
        sl = slice(sb * LANES, (sb + 1) * LANES)
        cur_s = cur[:, sl]
        xprev_ref[sb, pl.ds(2 * 8, q, stride=2), :] = cur_s
        acc = cb_ref[:, sl] + cur_s * cw_ref[CONV_WIDTH - 1:CONV_WIDTH, sl]
        for sh in range(1, CONV_WIDTH):
            shifted = xprev_ref[sb, pl.ds(2 * (8 - sh), q, stride=2), :]
            acc = acc + shifted * cw_ref[CONV_WIDTH - 1 - sh:CONV_WIDTH - sh, sl]
        xprev_ref[sb, pl.ds(0, 8, stride=2), :] = cur_s[q - 8:q]
        slabs.append(acc)
    conv = _silu(jnp.concatenate(slabs, axis=1))
    xs = conv[:, 0:SSM_WIDTH]

    tok = lax.broadcasted_iota(jnp.int32, (N_SSM_HEADS, q), 1)
    dt_in = chunk_rows(dt_ref).T[0:N_SSM_HEADS, :] + dtb_ref[...]
    dtv_t = jnp.maximum(dt_in, 0.0) + jnp.log1p(jnp.exp(-jnp.abs(dt_in)))
    dtv_t = jnp.where(chunk * q + tok < n_valid, dtv_t, 0.0)
    a_t = dtv_t * (-jnp.exp(alog_ref[...]))
    row_l = lax.broadcasted_iota(jnp.int32, (q, q), 0)
    lane_l = lax.broadcasted_iota(jnp.int32, (q, q), 1)
    tri = row_l >= lane_l
    a_cs_t = jnp.dot(a_t, (row_l <= lane_l).astype(F32), preferred_element_type=F32,
                     precision=lax.Precision.HIGHEST)
    a_cs = to_columns(a_cs_t)
    dtv = to_columns(dtv_t)
    low = lax.broadcasted_iota(jnp.int32, (q, PAIR), 1) < HEAD_DIM

    def per_pair(col0, col1):
        return jnp.where(low, col0, col1)

    ys = []
    for pr in range(N_SSM_HEADS // 2):
        grp = (2 * pr) // (N_SSM_HEADS // SSM_GROUPS)
        sl = slice(pr * PAIR, (pr + 1) * PAIR)
        bm = conv[:, SSM_WIDTH + grp * SSM_STATE:SSM_WIDTH + (grp + 1) * SSM_STATE]
        cm = conv[:, SSM_WIDTH + (SSM_GROUPS + grp) * SSM_STATE:SSM_WIDTH + (SSM_GROUPS + grp + 1) * SSM_STATE]
        bmb = bm.astype(BF16)
        cmb = cm.astype(BF16)
        cb = _dot_nt(cmb, bmb)
        h0, h1 = 2 * pr, 2 * pr + 1
        cs0, cs1 = a_cs[:, h0:h0 + 1], a_cs[:, h1:h1 + 1]
        tot0, tot1 = a_cs[q - 1:q, h0:h0 + 1], a_cs[q - 1:q, h1:h1 + 1]
        xs_p = xs[:, sl]
        xdt = xs_p * per_pair(dtv[:, h0:h0 + 1], dtv[:, h1:h1 + 1])
        xdt_b = xdt.astype(BF16)
        y_d = []
        for hh, cs in ((h0, cs0), (h1, cs1)):
            decay = jnp.exp(jnp.where(tri, cs - a_cs_t[hh:hh + 1, :], -jnp.inf))
            y_d.append(_dot((cb * decay).astype(BF16), xdt_b))
        y_diag = per_pair(y_d[0], y_d[1])
        to_end = per_pair(jnp.exp(tot0 - cs0), jnp.exp(tot1 - cs1))
        states = _dot_tn((xdt * to_end).astype(BF16), bmb)
        h_pair = h_ref[pr * PAIR:(pr + 1) * PAIR, :]
        y_off = _dot_nt(cmb, h_pair.astype(BF16)) * per_pair(jnp.exp(cs0), jnp.exp(cs1))
        chunk_decay = jnp.concatenate([jnp.broadcast_to(jnp.exp(tot0), (HEAD_DIM, SSM_STATE)),
                                       jnp.broadcast_to(jnp.exp(tot1), (HEAD_DIM, SSM_STATE))], axis=0)
        h_ref[pr * PAIR:(pr + 1) * PAIR, :] = h_pair * chunk_decay + states
        ys.append(y_diag + y_off + dskip_ref[:, sl] * xs_p)
    y = jnp.concatenate(ys, axis=1)

    u = y * _silu(chunk_rows(z_ref))
    gw = SSM_WIDTH // SSM_GROUPS
    normed = []
    for grp in range(SSM_GROUPS):
        ug = u[:, grp * gw:(grp + 1) * gw]
        normed.append(ug * lax.rsqrt(jnp.mean(ug * ug, axis=-1, keepdims=True) + EPS))
    out = jnp.concatenate(normed, axis=1) * g_ref[...]
    if rows_in >= q:
        y_ref[0, pl.ds(r0, q), :] = out
    else:
        y_ref[0] = out[0:rows_in]


def _ssd(xbc, z, dt, conv_prev8, h0, conv_w8, conv_b, dt_bias, a_log, d_skip, g_ssm, n_valid):
    b, l, _ = xbc.shape
    chunks = 4 if l % (4 * SSD_CHUNK) == 0 else 1
    rows_in = min(SSD_CHUNK * chunks, l)
    nc = l // rows_in
    row = lambda bi, c: (bi, c, 0)
    per_b = lambda bi, c: (bi, 0, 0)
    return pl.pallas_call(
        functools.partial(_ssd_body, rows_in=rows_in, chunks=chunks, n_valid=n_valid),
        grid=(b, nc),
        in_specs=[
            pl.BlockSpec((1, rows_in, CONV_CH), row),
            pl.BlockSpec((1, rows_in, SSM_WIDTH), row),
            pl.BlockSpec((1, rows_in, LANES), row),
            pl.BlockSpec((1, 8, CONV_CH), per_b),
            pl.BlockSpec((1, SSM_WIDTH, SSM_STATE), per_b),
            _const_spec((8, CONV_CH)),
            _const_spec((1, CONV_CH)),
            _const_spec((N_SSM_HEADS, SSD_CHUNK)),
            _const_spec((N_SSM_HEADS, SSD_CHUNK)),
            _const_spec((1, SSM_WIDTH)),
            _const_spec((1, SSM_WIDTH)),
        ],
        out_specs=(pl.BlockSpec((1, rows_in, SSM_WIDTH), row),
                   pl.BlockSpec((1, SSM_WIDTH, SSM_STATE), per_b)),
        out_shape=(jax.ShapeDtypeStruct((b, l, SSM_WIDTH), F32),
                   jax.ShapeDtypeStruct((b, SSM_WIDTH, SSM_STATE), F32)),
        scratch_shapes=[pltpu.VMEM((SSM_WIDTH, SSM_STATE), F32),
                        pltpu.VMEM((CONV_CH // LANES, 2 * (8 + SSD_CHUNK), LANES), F32)],
        compiler_params=_params("arbitrary", "arbitrary"),
        name="ssd",
    )(xbc, z, dt, conv_prev8, h0, conv_w8, conv_b, dt_bias, a_log, d_skip, g_ssm)


def _post1_body(x_ref, att_ref, y_ref, wout_ref, g_ref, wxq_ref, x1_ref, qx_ref):
    mixed = jnp.concatenate([att_ref[0, pr] for pr in range(N_PAIRS)] + [y_ref[0]], axis=1).astype(BF16)
    x1 = x_ref[0] + _dot(mixed, wout_ref[...])
    x1_ref[0] = x1
    qx_ref[0] = _dot(_rmsnorm(x1, g_ref[...]).astype(BF16), wxq_ref[...]).astype(BF16)


def _post1(x, att, y, w_out, g_xatt, w_xq, tm):
    b, l, _ = x.shape
    row = lambda bi, j: (bi, j, 0)
    full = pl.BlockSpec((1, tm, D_MODEL), row)
    return pl.pallas_call(
        _post1_body,
        grid=(b, l // tm),
        in_specs=[full,
                  pl.BlockSpec((1, N_PAIRS, tm, PAIR), lambda bi, j: (bi, 0, j, 0)),
                  pl.BlockSpec((1, tm, SSM_WIDTH), row),
                  _const_spec(w_out.shape), _const_spec((1, D_MODEL)), _const_spec(w_xq.shape)],
        out_specs=(full, full),
        out_shape=(jax.ShapeDtypeStruct((b, l, D_MODEL), F32), jax.ShapeDtypeStruct((b, l, D_MODEL), BF16)),
        compiler_params=_params("arbitrary", "arbitrary"),
        name="post1",
    )(x, att, y, w_out, g_xatt, w_xq)


def _mem_kv_body(m_ref, g_ref, wk_ref, wv_ref, k_ref, v_ref, kb_ref, vb_ref):
    h = _rmsnorm(m_ref[...], g_ref[...]).astype(BF16)
    k = _dot(h, wk_ref[...])
    v = _dot(h, wv_ref[...])
    k_ref[...] = k
    v_ref[...] = v
    kb_ref[...] = k.astype(BF16)
    vb_ref[...] = v.astype(BF16)


def _mem_kv(mem, g_mem, w_mk, w_mv, tm):
    n, _ = mem.shape
    row = pl.BlockSpec((tm, D_MODEL), lambda i: (i, 0))
    return pl.pallas_call(
        _mem_kv_body,
        grid=(n // tm,),
        in_specs=[row, _const_spec((1, D_MODEL)), _const_spec(w_mk.shape), _const_spec(w_mv.shape)],
        out_specs=(row, row, row, row),
        out_shape=(jax.ShapeDtypeStruct((n, D_MODEL), F32), jax.ShapeDtypeStruct((n, D_MODEL), F32),
                   jax.ShapeDtypeStruct((n, D_MODEL), BF16), jax.ShapeDtypeStruct((n, D_MODEL), BF16)),
        compiler_params=_params("arbitrary"),
        name="mem_kv",
    )(mem, g_mem, w_mk, w_mv)


def _xattn_body(q_ref, mk_ref, mv_ref, o_ref):
    scale = XATT_HEAD_DIM ** -0.5
    for gi in range(q_ref.shape[0]):
        for hd in range(N_XATT_HEADS):
            sl = slice(hd * XATT_HEAD_DIM, (hd + 1) * XATT_HEAD_DIM)
            s = _dot_nt(q_ref[gi, :, sl], mk_ref[gi, :, sl]) * scale
            p = jnp.exp(s - jnp.max(s, axis=-1, keepdims=True))
            den = jnp.sum(p, axis=-1, keepdims=True)
            o = _dot(p.astype(BF16), mv_ref[gi, :, sl]) / den
            o_ref[gi, :, sl] = o.astype(BF16)


def _xattn(qx, mem_k, mem_v, gb, tm):
    b, l, _ = qx.shape
    qspec = pl.BlockSpec((gb, tm, D_MODEL), lambda bi, j: (bi, j, 0))
    mspec = pl.BlockSpec((gb, N_MEM, D_MODEL), lambda bi, j: (bi, 0, 0))
    return pl.pallas_call(
        _xattn_body,
        grid=(b // gb, l // tm),
        in_specs=[qspec, mspec, mspec],
        out_specs=qspec,
        out_shape=jax.ShapeDtypeStruct((b, l, D_MODEL), BF16),
        compiler_params=_params("arbitrary", "arbitrary"),
        name="xattn",
    )(qx, mem_k, mem_v)


def _post2_body(x1_ref, o_ref, wxo_ref, gmlp_ref, wup_ref, wdown_ref, gfin_ref, y_ref):
    x2 = x1_ref[...] + _dot(o_ref[...], wxo_ref[...])
    hm = _rmsnorm(x2, gmlp_ref[...]).astype(BF16)
    acc = x2
    for c in range(D_FF // D_MODEL):
        sl = slice(c * D_MODEL, (c + 1) * D_MODEL)
        u = jnp.maximum(_dot(hm, wup_ref[:, sl]), 0.0)
        acc = acc + _dot((u * u).astype(BF16), wdown_ref[sl, :])
    y_ref[...] = _rmsnorm(acc, gfin_ref[...])


def _post2(x1, o, w_xo, g_mlp, w_up, w_down, g_final, tm):
    n, _ = x1.shape
    row = pl.BlockSpec((tm, D_MODEL), lambda i: (i, 0))
    return pl.pallas_call(
        _post2_body,
        grid=(n // tm,),
        in_specs=[row, row, _const_spec(w_xo.shape), _const_spec((1, D_MODEL)), _const_spec(w_up.shape),
                  _const_spec(w_down.shape), _const_spec((1, D_MODEL))],
        out_specs=row,
        out_shape=jax.ShapeDtypeStruct((n, D_MODEL), F32),
        compiler_params=_params("arbitrary"),
        name="post2",
    )(x1, o, w_xo, g_mlp, w_up, w_down, g_final)


def _rope_tables(pos):
    half = HEAD_DIM // 2
    inv = ROPE_THETA ** (-jnp.arange(half, dtype=F32) * 2.0 / HEAD_DIM)
    ang = pos.astype(F32)[:, None] * inv[None, :]
    c, s = jnp.cos(ang), jnp.sin(ang)
    return jnp.concatenate([c, c, c, c], axis=1), jnp.concatenate([-s, s, -s, s], axis=1)


def _pad_lanes(t):
    return jnp.pad(t, ((0, 0), (0, LANES - t.shape[1])))


def kernel(x_prompt, x_sample, cache_win_k, cache_win_v, state_conv, state_ssm, cache_mem_k, cache_mem_v,
           mem_prompt, g_mix, w_in, conv_w, conv_b, dt_bias, a_log, d_skip, g_ssm, w_out, g_xatt, g_mem,
           w_xq, w_mk, w_mv, w_xo, g_mlp, w_up, w_down, g_final):
    depth = w_in.shape[0]
    assert depth == 1, "kernel is written for the single-layer trunk of this problem"
    bp, s_len, _ = x_prompt.shape
    bs, t_new, _ = x_sample.shape
    lb = cache_win_k.shape[2]
    lw = min(MAX_WINDOW, s_len)
    n_proj = ATT_WIDTH * 3 + SSM_WIDTH + CONV_CH

    li = 0
    row = lambda t: t[li].reshape(1, -1)
    w_main = w_in[li, :, :n_proj].astype(BF16)
    w_dt = _pad_lanes(w_in[li, :, n_proj:]).astype(BF16)
    conv_w8 = jnp.pad(conv_w[li], ((0, 8 - CONV_WIDTH), (0, 0)))
    per_head_rows = lambda t: jnp.broadcast_to(t[li][:, None], (N_SSM_HEADS, SSD_CHUNK))
    dt_b = per_head_rows(dt_bias)
    a_lg = per_head_rows(a_log)
    d_sk = jnp.repeat(d_skip[li], HEAD_DIM).reshape(1, -1)
    w_o, w_q, w_k, w_v, w_x = (t[li].astype(BF16) for t in (w_out, w_xq, w_mk, w_mv, w_xo))
    w_u, w_d = w_up[li].astype(BF16), w_down[li].astype(BF16)
    g_fin = g_final.reshape(1, -1)

    cos_p, sin_p = _rope_tables(jnp.arange(s_len, dtype=jnp.int32))
    q, k, v, k32, v32, z, xbc, dt = _in_proj(x_prompt, row(g_mix), w_main, w_dt, cos_p, sin_p, lw, ROW_TILE)
    att = _attn_prompt(q, k, v)
    y_ssm, ssm_p = _ssd(xbc, z, dt, jnp.zeros((bp, 8, CONV_CH), F32),
                        jnp.zeros((bp, SSM_WIDTH, SSM_STATE), F32), conv_w8, row(conv_b), dt_b, a_lg, d_sk,
                        row(g_ssm), s_len)
    mk, mv, mk_b, mv_b = _mem_kv(mem_prompt.reshape(bp * N_MEM, D_MODEL), row(g_mem), w_k, w_v, ROW_TILE)
    x1, qx = _post1(x_prompt, att, y_ssm, w_o, row(g_xatt), w_q, ROW_TILE)
    o = _xattn(qx, mk_b.reshape(bp, N_MEM, D_MODEL), mv_b.reshape(bp, N_MEM, D_MODEL), 1, ROW_TILE)
    y_prompt = _post2(x1.reshape(bp * s_len, D_MODEL), o.reshape(bp * s_len, D_MODEL), w_x, row(g_mlp),
                      w_u, w_d, g_fin, ROW_TILE)
    y_prompt = y_prompt.reshape(bp, s_len, D_MODEL)
    win_k_p = k32.reshape(1, bp, lw, N_ATT_HEADS, HEAD_DIM)
    win_v_p = v32.reshape(1, bp, lw, N_ATT_HEADS, HEAD_DIM)
    conv_p = xbc[:, s_len - (CONV_WIDTH - 1):][None]
    ssm_p = ssm_p.reshape(1, bp, N_SSM_HEADS, HEAD_DIM, SSM_STATE)
    mk_p = mk.reshape(1, bp, N_MEM, N_XATT_HEADS, XATT_HEAD_DIM)
    mv_p = mv.reshape(1, bp, N_MEM, N_XATT_HEADS, XATT_HEAD_DIM)

    r = SAMPLE_ROWS
    n_s = bs * r
    xs_pad = jnp.pad(x_sample, ((0, 0), (0, r - t_new), (0, 0))).reshape(1, n_s, D_MODEL)
    pos_s = PAST_LEN + (jnp.arange(n_s, dtype=jnp.int32) % r)
    cos_s, sin_s = _rope_tables(pos_s)
    q, _, _, k32, v32, z, xbc, dt = _in_proj(xs_pad, row(g_mix), w_main, w_dt, cos_s, sin_s, n_s, ROW_TILE)
    to_minor = lambda c: jnp.transpose(c[li].reshape(bs, lb, ATT_WIDTH), (0, 2, 1))
    att, kt_s, vt_s = _attn_sample(q, k32, v32, to_minor(cache_win_k), to_minor(cache_win_v), r, t_new)
    per_seq = lambda t: t.reshape(bs, r, t.shape[-1])
    conv_prev8 = jnp.pad(state_conv[li], ((0, 0), (8 - (CONV_WIDTH - 1), 0), (0, 0)))
    xbc_s = per_seq(xbc)
    y_ssm, ssm_s = _ssd(xbc_s, per_seq(z), per_seq(dt), conv_prev8,
                        state_ssm[li].reshape(bs, SSM_WIDTH, SSM_STATE), conv_w8, row(conv_b), dt_b, a_lg, d_sk,
                        row(g_ssm), t_new)
    x1, qx = _post1(xs_pad, att, y_ssm.reshape(1, n_s, SSM_WIDTH), w_o, row(g_xatt), w_q, ROW_TILE)
    o = _xattn(qx.reshape(bs, r, D_MODEL), cache_mem_k[li].reshape(bs, N_MEM, D_MODEL).astype(BF16),
               cache_mem_v[li].reshape(bs, N_MEM, D_MODEL).astype(BF16), 4, r)
    y_s = _post2(x1.reshape(n_s, D_MODEL), o.reshape(n_s, D_MODEL), w_x, row(g_mlp), w_u, w_d, g_fin, ROW_TILE)
    y_sample = y_s.reshape(bs, r, D_MODEL)[:, :t_new]
    from_minor = lambda c: jnp.transpose(c, (0, 2, 1)).reshape(1, bs, lb, N_ATT_HEADS, HEAD_DIM)
    win_k_s, win_v_s = from_minor(kt_s), from_minor(vt_s)
    conv_s = jnp.concatenate([state_conv[li], xbc_s[:, :t_new]], axis=1)[:, -(CONV_WIDTH - 1):][None]
    ssm_s = ssm_s.reshape(1, bs, N_SSM_HEADS, HEAD_DIM, SSM_STATE)

    return (y_prompt, y_sample, win_k_p, win_v_p, conv_p, ssm_p, mk_p, mv_p,
            win_k_s, win_v_s, conv_s, ssm_s)
```

```python
import functools

import jax
import jax.numpy as jnp
import numpy as np
from jax import lax
from jax.experimental import pallas as pl
from jax.experimental.pallas import tpu as pltpu

F32 = jnp.float32
BF16 = jnp.bfloat16

D_MODEL = 1024
PAST_LEN = 8192
HEAD_DIM = 64
ATT_WIDTH = 512
N_ATT_HEADS = 8
DILATED_BRANCHES = ((128, 1), (512, 4), (2048, 16))
MAX_WINDOW = 2048
ROPE_THETA = 10000.0
SSM_WIDTH = 512
N_SSM_HEADS = 8
SSM_GROUPS = 2
SSM_STATE = 128
CONV_WIDTH = 4
SSD_CHUNK = 128
CONV_CH = 1024
N_MEM = 256
N_XATT_HEADS = 4
XATT_HEAD_DIM = 256
D_FF = 4096
EPS = 1e-6

LANES = 128
PAIR = 2 * HEAD_DIM
N_PAIRS = ATT_WIDTH // PAIR
BRANCH_BLOCK = 128
ATT_TILE = 2048
UNITS_PER_ITER = 4
ROW_TILE = 512
SAMPLE_ROWS = 16
VMEM_LIMIT = 56 * 1024 * 1024


def _params(*sem):
    return pltpu.CompilerParams(dimension_semantics=sem, vmem_limit_bytes=VMEM_LIMIT)


def _const_spec(shape):
    return pl.BlockSpec(shape, lambda *_: (0,) * len(shape), pipeline_mode=pl.Buffered(1))


def _rmsnorm(x, g):
    return x * lax.rsqrt(jnp.mean(x * x, axis=-1, keepdims=True) + EPS) * g


def _dot(a, b):
    return jnp.dot(a, b, preferred_element_type=F32)


def _dot_nt(a, b):
    return lax.dot_general(a, b, (((1,), (1,)), ((), ())), preferred_element_type=F32)


def _dot_tn(a, b):
    return lax.dot_general(a, b, (((0,), (0,)), ((), ())), preferred_element_type=F32)


def _in_proj_body(x_ref, g_ref, w_ref, wdt_ref, cos_ref, sin_ref,
                  q_ref, k_ref, v_ref, k32_ref, v32_ref, z_ref, xbc_ref, dt_ref):
    h = _rmsnorm(x_ref[0], g_ref[...]).astype(BF16)
    tm = h.shape[0]
    cos = jnp.concatenate([cos_ref[...]] * N_PAIRS, axis=1)
    sin = jnp.concatenate([sin_ref[...]] * N_PAIRS, axis=1)
    lane = lax.broadcasted_iota(jnp.int32, (tm, ATT_WIDTH), 1)
    first_half = (lane & (HEAD_DIM // 2)) == 0

    def rope(t):
        partner = jnp.where(first_half,
                            pltpu.roll(t, ATT_WIDTH - HEAD_DIM // 2, axis=1),
                            pltpu.roll(t, HEAD_DIM // 2, axis=1))
        return t * cos + partner * sin

    q = rope(_dot(h, w_ref[:, 0:512]))
    k = rope(_dot(h, w_ref[:, 512:1024]))
    v = _dot(h, w_ref[:, 1024:1536])
    for pr in range(N_PAIRS):
        sl = slice(pr * PAIR, (pr + 1) * PAIR)
        q_ref[0, pr] = q[:, sl]
        k_ref[0, pr] = k[:, sl]
        v_ref[0, pr] = v[:, sl]
    k32_ref[0] = k
    v32_ref[0] = v
    z_ref[0] = _dot(h, w_ref[:, 1536:2048])
    xbc_ref[0] = _dot(h, w_ref[:, 2048:3072])
    dt_ref[0] = _dot(h, wdt_ref[...])


def _in_proj(x, g, w_main, w_dt, cos, sin, window, tm):
    b, l, _ = x.shape
    nj = l // tm
    j0 = (l - window) // tm
    row = lambda bi, j: (bi, j, 0)
    slab = lambda bi, j: (bi, 0, j, 0)
    win = lambda bi, j: (bi, jnp.maximum(j - j0, 0), 0)
    slab_shape = jax.ShapeDtypeStruct((b, N_PAIRS, l, PAIR), F32)
    slab_spec = pl.BlockSpec((1, N_PAIRS, tm, PAIR), slab)
    outs = (
        slab_shape, slab_shape, slab_shape,
        jax.ShapeDtypeStruct((b, window, ATT_WIDTH), F32),
        jax.ShapeDtypeStruct((b, window, ATT_WIDTH), F32),
        jax.ShapeDtypeStruct((b, l, SSM_WIDTH), F32),
        jax.ShapeDtypeStruct((b, l, CONV_CH), F32),
        jax.ShapeDtypeStruct((b, l, LANES), F32),
    )
    return pl.pallas_call(
        _in_proj_body,
        grid=(b, nj),
        in_specs=[
            pl.BlockSpec((1, tm, D_MODEL), row),
            _const_spec((1, D_MODEL)),
            _const_spec(w_main.shape),
            _const_spec(w_dt.shape),
            pl.BlockSpec((tm, LANES), lambda bi, j: (j, 0)),
            pl.BlockSpec((tm, LANES), lambda bi, j: (j, 0)),
        ],
        out_specs=(
            slab_spec, slab_spec, slab_spec,
            pl.BlockSpec((1, tm, ATT_WIDTH), win),
            pl.BlockSpec((1, tm, ATT_WIDTH), win),
            pl.BlockSpec((1, tm, SSM_WIDTH), row),
            pl.BlockSpec((1, tm, CONV_CH), row),
            pl.BlockSpec((1, tm, LANES), row),
        ),
        out_shape=outs,
        compiler_params=_params("arbitrary", "arbitrary"),
        name="in_proj",
    )(x, g, w_main, w_dt, cos, sin)


def _attn_prompt_body(q_ref, kc_ref, kp_ref, vc_ref, vp_ref, o_ref, r0_ref, r1_ref, m0_ref, m1_ref):
    t = pl.program_id(2)
    n = BRANCH_BLOCK
    tile = ATT_TILE

    row = lax.broadcasted_iota(jnp.int32, (2 * n, 2 * n), 0) & (n - 1)
    col = lax.broadcasted_iota(jnp.int32, (2 * n, 2 * n), 1)
    bias_cur = jnp.where(col - n <= row, 0.0, -jnp.inf)
    bias = jnp.where(col < n, jnp.where(col >= row, 0.0, -jnp.inf), bias_cur)
    bias_first = jnp.where(col < n, -jnp.inf, bias_cur)
    low = lax.broadcasted_iota(jnp.int32, (n, PAIR), 1) < HEAD_DIM
    low2 = lax.broadcasted_iota(jnp.int32, (2 * n, PAIR), 1) < HEAD_DIM
    scale = HEAD_DIM ** -0.5

    def rows(start, stride):
        return pl.ds(start, n) if stride == 1 else pl.ds(start, n, stride=stride)

    def prev_block(cur_ref, prev_ref, q0, stride, where):
        if isinstance(where, str):
            if where == "cur":
                return cur_ref[rows(q0 - n * stride, stride), :]
            return prev_ref[rows(tile + q0 - n * stride, stride), :]
        return jnp.where(where, cur_ref[rows(jnp.maximum(q0 - n, 0), 1), :], prev_ref[tile - n:tile, :])

    def run_units(units, stride, merge, last=False):
        loaded = []
        for q0, where, _ in units:
            qv = q_ref[rows(q0, stride), :] * scale
            kk = jnp.concatenate([prev_block(kc_ref, kp_ref, q0, stride, where),
                                  kc_ref[rows(q0, stride), :]], axis=0)
            vv = jnp.concatenate([prev_block(vc_ref, vp_ref, q0, stride, where),
                                  vc_ref[rows(q0, stride), :]], axis=0)
            qm = jnp.concatenate([jnp.where(low, qv, 0.0), jnp.where(low, 0.0, qv)], axis=0).astype(BF16)
            loaded.append((qm, kk.astype(BF16),
                           jnp.where(low2, vv, 1.0).astype(BF16), jnp.where(low2, 1.0, vv).astype(BF16)))
        scores = []
        for (qm, kk, _, _), (_, _, has_prev) in zip(loaded, units):
            b = bias if has_prev is None else jnp.where(has_prev, bias, bias_first)
            scores.append(_dot_nt(qm, kk) + b)
        m_new, alpha = [], []
        for s, (q0, _, _) in zip(scores, units):
            m_row = jnp.max(s, axis=-1, keepdims=True)
            if merge:
                m_old = jnp.concatenate([m0_ref[rows(q0, stride), :], m1_ref[rows(q0, stride), :]], axis=0)
                m = jnp.maximum(m_old, m_row)
                alpha.append(jnp.exp(m_old - m))
            else:
                m = jnp.broadcast_to(m_row, (2 * n, PAIR))
            m_new.append(m)
        probs = [jnp.exp(s - jnp.concatenate([m, m], axis=1)).astype(BF16) for s, m in zip(scores, m_new)]
        for i, (q0, _, _) in enumerate(units):
            _, _, va0, va1 = loaded[i]
            acc0 = _dot(probs[i][0:n], va0)
            acc1 = _dot(probs[i][n:], va1)
            if merge:
                acc0 = acc0 + alpha[i][0:n] * r0_ref[rows(q0, stride), :]
                acc1 = acc1 + alpha[i][n:] * r1_ref[rows(q0, stride), :]
            if last:
                num = jnp.where(low, acc0, acc1)
                den = pltpu.roll(jnp.where(low, acc1, acc0), HEAD_DIM, axis=1)
                o_ref[rows(q0, stride), :] = num / den
            else:
                r0_ref[rows(q0, stride), :] = acc0
                r1_ref[rows(q0, stride), :] = acc1
                m0_ref[rows(q0, stride), :] = m_new[i][0:n]
                m1_ref[rows(q0, stride), :] = m_new[i][n:]

    g_units = UNITS_PER_ITER
    not_first_tile = t > 0

    def branch_d16(it, carry):
        units = [(it * g_units + g, "prev", not_first_tile) for g in range(g_units)]
        run_units(units, 16, merge=False)
        return carry

    lax.fori_loop(0, 16 // g_units, branch_d16, 0)

    def branch_d4(r, carry):
        units = [(i * (4 * n) + r, "prev" if i == 0 else "cur", not_first_tile if i == 0 else None)
                 for i in range(tile // (4 * n))]
        run_units(units, 4, merge=True)
        return carry

    lax.fori_loop(0, 4, branch_d4, 0)

    def branch_d1(it, carry):
        base = pl.multiple_of(it * (g_units * n), g_units * n)
        units = [(base, it > 0, jnp.logical_or(not_first_tile, it > 0))]
        units += [(base + g * n, "cur", None) for g in range(1, g_units)]
        run_units(units, 1, merge=True, last=True)
        return carry

    lax.fori_loop(0, tile // (g_units * n), branch_d1, 0)


def _attn_prompt(q, k, v):
    b, npair, s, w = q.shape
    assert [d for _, d in DILATED_BRANCHES] == [1, 4, 16] and s % ATT_TILE == 0
    tile = ATT_TILE
    cur = pl.BlockSpec((None, None, tile, w), lambda bi, p, t: (bi, p, t, 0))
    prev = pl.BlockSpec((None, None, tile, w), lambda bi, p, t: (bi, p, jnp.maximum(t - 1, 0), 0))
    return pl.pallas_call(
        _attn_prompt_body,
        grid=(b, npair, s // tile),
        in_specs=[cur, cur, prev, cur, prev],
        out_specs=cur,
        out_shape=jax.ShapeDtypeStruct((b, npair, s, w), F32),
        scratch_shapes=[pltpu.VMEM((tile, w), F32)] * 4,
        compiler_params=_params("arbitrary", "arbitrary", "arbitrary"),
        name="attn_prompt",
    )(q, k, k, v, v)


def _branch_count(dist):
    cnt = jnp.zeros(dist.shape, F32)
    for window, dil in DILATED_BRANCHES:
        hit = (dist >= 0) & (dist <= window) & ((dist & (dil - 1)) == 0)
        cnt = cnt + hit.astype(F32)
    return cnt


def _attn_sample_body(q_ref, kn_ref, vn_ref, kt_ref, vt_ref, att_ref, kto_ref, vto_ref, *, n_new):
    lb = kt_ref.shape[1]
    rows = q_ref.shape[1]
    k_new = kn_ref[...]
    v_new = vn_ref[...]

    tail_lane = lax.broadcasted_iota(jnp.int32, (64, LANES), 1) >= LANES - n_new
    for new, src_ref, dst_ref in ((k_new, kt_ref, kto_ref), (v_new, vt_ref, vto_ref)):
        last8 = pltpu.roll(new[0:8], 8 - n_new, axis=0)
        padded = jnp.concatenate([jnp.zeros((LANES - 8, ATT_WIDTH), F32), last8], axis=0)
        new_t = jnp.concatenate([padded[:, p * LANES:(p + 1) * LANES].T for p in range(N_PAIRS)], axis=0)
        for c in range(ATT_WIDTH // 64):
            shifted = pltpu.roll(src_ref[c * 64:(c + 1) * 64, :], lb - n_new, axis=1)
            dst_ref[c * 64:(c + 1) * 64, 0:lb - LANES] = shifted[:, 0:lb - LANES]
            dst_ref[c * 64:(c + 1) * 64, lb - LANES:lb] = jnp.where(
                tail_lane, new_t[c * 64:(c + 1) * 64], shifted[:, lb - LANES:lb])

    t_c = lax.broadcasted_iota(jnp.int32, (rows, lb), 0)
    i_c = lax.broadcasted_iota(jnp.int32, (rows, lb), 1)
    cnt_c = _branch_count(lb + t_c - i_c)
    t_n = lax.broadcasted_iota(jnp.int32, (rows, rows), 0)
    i_n = lax.broadcasted_iota(jnp.int32, (rows, rows), 1)
    cnt_n = jnp.where(i_n < n_new, _branch_count(t_n - i_n), 0.0)
    low = lax.broadcasted_iota(jnp.int32, (rows, PAIR), 1) < HEAD_DIM
    scale = HEAD_DIM ** -0.5
    for pr in range(N_PAIRS):
        sl = slice(pr * PAIR, (pr + 1) * PAIR)
        qp = q_ref[pr]
        ktp = kt_ref[sl, :].astype(BF16)
        vtp = vt_ref[sl, :].astype(BF16)
        knp = k_new[:, sl].astype(BF16)
        vnp = v_new[:, sl].astype(BF16)
        o_h = []
        for hh in range(2):
            qm = jnp.where(low if hh == 0 else jnp.logical_not(low), qp, 0.0).astype(BF16)
            s_c = jnp.where(cnt_c > 0, _dot(qm, ktp) * scale, -jnp.inf)
            s_n = jnp.where(cnt_n > 0, _dot_nt(qm, knp) * scale, -jnp.inf)
            m = jnp.maximum(jnp.max(s_c, axis=-1, keepdims=True), jnp.max(s_n, axis=-1, keepdims=True))
            p_c = cnt_c * jnp.exp(s_c - m)
            p_n = cnt_n * jnp.exp(s_n - m)
            den = jnp.sum(p_c, axis=-1, keepdims=True) + jnp.sum(p_n, axis=-1, keepdims=True)
            pv = _dot_nt(p_c.astype(BF16), vtp) + _dot(p_n.astype(BF16), vnp)
            o_h.append(pv / den)
        att_ref[pr] = jnp.where(low, o_h[0], o_h[1])


def _attn_sample(q, k_new, v_new, kt_cache, vt_cache, rows, n_new):
    b, w, lb = kt_cache.shape
    assert n_new <= 8 and lb % LANES == 0
    slab = pl.BlockSpec((None, N_PAIRS, rows, PAIR), lambda bi: (0, 0, bi, 0))
    new = pl.BlockSpec((None, rows, w), lambda bi: (0, bi, 0))
    cache = pl.BlockSpec((None, w, lb), lambda bi: (bi, 0, 0))
    return pl.pallas_call(
        functools.partial(_attn_sample_body, n_new=n_new),
        grid=(b,),
        in_specs=[slab, new, new, cache, cache],
        out_specs=(slab, cache, cache),
        out_shape=(jax.ShapeDtypeStruct(q.shape, F32),
                   jax.ShapeDtypeStruct((b, w, lb), F32),
                   jax.ShapeDtypeStruct((b, w, lb), F32)),
        compiler_params=_params("arbitrary"),
        name="attn_sample",
    )(q, k_new, v_new, kt_cache, vt_cache)


def _ssd_body(xbc_ref, z_ref, dt_ref, cprev_ref, h0_ref, cw_ref, cb_ref, dtb_ref, alog_ref, dskip_ref, g_ref,
              y_ref, hout_ref, h_ref, xprev_ref, *, rows_in, chunks, n_valid):
    q = SSD_CHUNK

    @pl.when(pl.program_id(1) == 0)
    def _():
        h_ref[...] = h0_ref[0]
        xprev_ref[...] = cprev_ref[0]

    def one_chunk(ck, carry):
        _ssd_chunk(pl.program_id(1) * chunks + ck, pl.multiple_of(ck * q, q) if rows_in >= q else 0,
                   xbc_ref, z_ref, dt_ref, cw_ref, cb_ref, dtb_ref, alog_ref, dskip_ref, g_ref, y_ref,
                   h_ref, xprev_ref, rows_in=rows_in, n_valid=n_valid)
        return carry

    lax.fori_loop(0, chunks, one_chunk, 0)
    hout_ref[0] = h_ref[...]


def _silu(x):
    half = 0.5 * x
    return half + half * jnp.tanh(half)


def _ssd_chunk(chunk, r0, xbc_ref, z_ref, dt_ref, cw_ref, cb_ref, dtb_ref, alog_ref, dskip_ref, g_ref, y_ref,
               h_ref, xprev_ref, *, rows_in, n_valid):
    q = SSD_CHUNK

    def chunk_rows(ref):
        if rows_in >= q:
            return ref[0, pl.ds(r0, q), :]
        t = ref[0]
        return jnp.concatenate([t, jnp.zeros((q - rows_in, t.shape[1]), t.dtype)], axis=0)

    def to_columns(t8):
        return jnp.concatenate([t8, jnp.zeros((LANES - 8, q), F32)], axis=0).T

    cur = chunk_rows(xbc_ref)
    prev8 = xprev_ref[...]
    i8 = lax.broadcasted_iota(jnp.int32, (8, CONV_CH), 0)
    acc = cb_ref[...] + cur * cw_ref[CONV_WIDTH - 1:CONV_WIDTH, :]
    for sh in range(1, CONV_WIDTH):
        rolled = pltpu.roll(cur, sh, axis=0)
        head = jnp.where(i8 < sh, pltpu.roll(prev8, sh, axis=0), rolled[0:8])
        shifted = jnp.concatenate([head, rolled[8:]], axis=0)
        acc = acc + shifted * cw_ref[CONV_WIDTH - 1 - sh:CONV_WIDTH - sh, :]
    xprev_ref[...] = cur[q - 8:q]
    conv = _silu(acc)
    xs = conv[:, 0:SSM_WIDTH]

    tok = lax.broadcasted_iota(jnp.int32, (N_SSM_HEADS, q), 1)
    dt_in = chunk_rows(dt_ref).T[0:N_SSM_HEADS, :] + dtb_ref[...]
    dtv_t = jnp.maximum(dt_in, 0.0) + jnp.log1p(jnp.exp(-jnp.abs(dt_in)))
    dtv_t = jnp.where(chunk * q + tok < n_valid, dtv_t, 0.0)
    a_t = dtv_t * (-jnp.exp(alog_ref[...]))
    row_l = lax.broadcasted_iota(jnp.int32, (q, q), 0)
    lane_l = lax.broadcasted_iota(jnp.int32, (q, q), 1)
    tri = row_l >= lane_l
    a_cs_t = jnp.dot(a_t, (row_l <= lane_l).astype(F32), preferred_element_type=F32,
                     precision=lax.Precision.HIGHEST)
    a_cs = to_columns(a_cs_t)
    dtv = to_columns(dtv_t)
    low = lax.broadcasted_iota(jnp.int32, (q, PAIR), 1) < HEAD_DIM

    def per_pair(col0, col1):
        return jnp.where(low, col0, col1)

    ys = []
    for pr in range(N_SSM_HEADS // 2):
        grp = (2 * pr) // (N_SSM_HEADS // SSM_GROUPS)
        sl = slice(pr * PAIR, (pr + 1) * PAIR)
        bm = conv[:, SSM_WIDTH + grp * SSM_STATE:SSM_WIDTH + (grp + 1) * SSM_STATE]
        cm = conv[:, SSM_WIDTH + (SSM_GROUPS + grp) * SSM_STATE:SSM_WIDTH + (SSM_GROUPS + grp + 1) * SSM_STATE]
        bmb = bm.astype(BF16)
        cmb = cm.astype(BF16)
        cb = _dot_nt(cmb, bmb)
        h0, h1 = 2 * pr, 2 * pr + 1
        cs0, cs1 = a_cs[:, h0:h0 + 1], a_cs[:, h1:h1 + 1]
        tot0, tot1 = a_cs[q - 1:q, h0:h0 + 1], a_cs[q - 1:q, h1:h1 + 1]
        xs_p = xs[:, sl]
        xdt = xs_p * per_pair(dtv[:, h0:h0 + 1], dtv[:, h1:h1 + 1])
        xdt_b = xdt.astype(BF16)
        y_d = []
        for hh, cs in ((h0, cs0), (h1, cs1)):
            decay = jnp.exp(jnp.where(tri, cs - a_cs_t[hh:hh + 1, :], -jnp.inf))
            y_d.append(_dot((cb * decay).astype(BF16), xdt_b))
        y_diag = per_pair(y_d[0], y_d[1])
        to_end = per_pair(jnp.exp(tot0 - cs0), jnp.exp(tot1 - cs1))
        states = _dot_tn((xdt * to_end).astype(BF16), bmb)
        h_pair = h_ref[pr * PAIR:(pr + 1) * PAIR, :]
        y_off = _dot_nt(cmb, h_pair.astype(BF16)) * per_pair(jnp.exp(cs0), jnp.exp(cs1))
        chunk_decay = jnp.concatenate([jnp.broadcast_to(jnp.exp(tot0), (HEAD_DIM, SSM_STATE)),
                                       jnp.broadcast_to(jnp.exp(tot1), (HEAD_DIM, SSM_STATE))], axis=0)
        h_ref[pr * PAIR:(pr + 1) * PAIR, :] = h_pair * chunk_decay + states
        ys.append(y_diag + y_off + dskip_ref[:, sl] * xs_p)
    y = jnp.concatenate(ys, axis=1)

    u = y * _silu(chunk_rows(z_ref))
    gw = SSM_WIDTH // SSM_GROUPS
    normed = []
    for grp in range(SSM_GROUPS):
        ug = u[:, grp * gw:(grp + 1) * gw]
        normed.append(ug * lax.rsqrt(jnp.mean(ug * ug, axis=-1, keepdims=True) + EPS))
    out = jnp.concatenate(normed, axis=1) * g_ref[...]
    if rows_in >= q:
        y_ref[0, pl.ds(r0, q), :] = out
    else:
        y_ref[0] = out[0:rows_in]


def _ssd(xbc, z, dt, conv_prev8, h0, conv_w8, conv_b, dt_bias, a_log, d_skip, g_ssm, n_valid):
    b, l, _ = xbc.shape
    chunks = 4 if l % (4 * SSD_CHUNK) == 0 else 1
    rows_in = min(SSD_CHUNK * chunks, l)
    nc = l // rows_in
    row = lambda bi, c: (bi, c, 0)
    per_b = lambda bi, c: (bi, 0, 0)
    return pl.pallas_call(
        functools.partial(_ssd_body, rows_in=rows_in, chunks=chunks, n_valid=n_valid),
        grid=(b, nc),
        in_specs=[
            pl.BlockSpec((1, rows_in, CONV_CH), row),
            pl.BlockSpec((1, rows_in, SSM_WIDTH), row),
            pl.BlockSpec((1, rows_in, LANES), row),
            pl.BlockSpec((1, 8, CONV_CH), per_b),
            pl.BlockSpec((1, SSM_WIDTH, SSM_STATE), per_b),
            _const_spec((8, CONV_CH)),
            _const_spec((1, CONV_CH)),
            _const_spec((N_SSM_HEADS, SSD_CHUNK)),
            _const_spec((N_SSM_HEADS, SSD_CHUNK)),
            _const_spec((1, SSM_WIDTH)),
            _const_spec((1, SSM_WIDTH)),
        ],
        out_specs=(pl.BlockSpec((1, rows_in, SSM_WIDTH), row),
                   pl.BlockSpec((1, SSM_WIDTH, SSM_STATE), per_b)),
        out_shape=(jax.ShapeDtypeStruct((b, l, SSM_WIDTH), F32),
                   jax.ShapeDtypeStruct((b, SSM_WIDTH, SSM_STATE), F32)),
        scratch_shapes=[pltpu.VMEM((SSM_WIDTH, SSM_STATE), F32), pltpu.VMEM((8, CONV_CH), F32)],
        compiler_params=_params("arbitrary", "arbitrary"),
        name="ssd",
    )(xbc, z, dt, conv_prev8, h0, conv_w8, conv_b, dt_bias, a_log, d_skip, g_ssm)


def _post1_body(x_ref, att_ref, y_ref, wout_ref, g_ref, wxq_ref, x1_ref, qx_ref):
    mixed = jnp.concatenate([att_ref[0, pr] for pr in range(N_PAIRS)] + [y_ref[0]], axis=1).astype(BF16)
    x1 = x_ref[0] + _dot(mixed, wout_ref[...])
    x1_ref[0] = x1
    qx_ref[0] = _dot(_rmsnorm(x1, g_ref[...]).astype(BF16), wxq_ref[...]).astype(BF16)


def _post1(x, att, y, w_out, g_xatt, w_xq, tm):
    b, l, _ = x.shape
    row = lambda bi, j: (bi, j, 0)
    full = pl.BlockSpec((1, tm, D_MODEL), row)
    return pl.pallas_call(
        _post1_body,
        grid=(b, l // tm),
        in_specs=[full,
                  pl.BlockSpec((1, N_PAIRS, tm, PAIR), lambda bi, j: (bi, 0, j, 0)),
                  pl.BlockSpec((1, tm, SSM_WIDTH), row),
                  _const_spec(w_out.shape), _const_spec((1, D_MODEL)), _const_spec(w_xq.shape)],
        out_specs=(full, full),
        out_shape=(jax.ShapeDtypeStruct((b, l, D_MODEL), F32), jax.ShapeDtypeStruct((b, l, D_MODEL), BF16)),
        compiler_params=_params("arbitrary", "arbitrary"),
        name="post1",
    )(x, att, y, w_out, g_xatt, w_xq)


def _mem_kv_body(m_ref, g_ref, wk_ref, wv_ref, k_ref, v_ref, kb_ref, vb_ref):
    h = _rmsnorm(m_ref[...], g_ref[...]).astype(BF16)
    k = _dot(h, wk_ref[...])
    v = _dot(h, wv_ref[...])
    k_ref[...] = k
    v_ref[...] = v
    kb_ref[...] = k.astype(BF16)
    vb_ref[...] = v.astype(BF16)


def _mem_kv(mem, g_mem, w_mk, w_mv, tm):
    n, _ = mem.shape
    row = pl.BlockSpec((tm, D_MODEL), lambda i: (i, 0))
    return pl.pallas_call(
        _mem_kv_body,
        grid=(n // tm,),
        in_specs=[row, _const_spec((1, D_MODEL)), _const_spec(w_mk.shape), _const_spec(w_mv.shape)],
        out_specs=(row, row, row, row),
        out_shape=(jax.ShapeDtypeStruct((n, D_MODEL), F32), jax.ShapeDtypeStruct((n, D_MODEL), F32),
                   jax.ShapeDtypeStruct((n, D_MODEL), BF16), jax.ShapeDtypeStruct((n, D_MODEL), BF16)),
        compiler_params=_params("arbitrary"),
        name="mem_kv",
    )(mem, g_mem, w_mk, w_mv)


def _xattn_body(q_ref, mk_ref, mv_ref, o_ref):
    scale = XATT_HEAD_DIM ** -0.5
    for gi in range(q_ref.shape[0]):
        for hd in range(N_XATT_HEADS):
            sl = slice(hd * XATT_HEAD_DIM, (hd + 1) * XATT_HEAD_DIM)
            s = _dot_nt(q_ref[gi, :, sl], mk_ref[gi, :, sl].astype(BF16)) * scale
            p = jnp.exp(s - jnp.max(s, axis=-1, keepdims=True))
            den = jnp.sum(p, axis=-1, keepdims=True)
            o = _dot(p.astype(BF16), mv_ref[gi, :, sl].astype(BF16)) / den
            o_ref[gi, :, sl] = o.astype(BF16)


def _xattn(qx, mem_k, mem_v, gb, tm):
    b, l, _ = qx.shape
    qspec = pl.BlockSpec((gb, tm, D_MODEL), lambda bi, j: (bi, j, 0))
    mspec = pl.BlockSpec((gb, N_MEM, D_MODEL), lambda bi, j: (bi, 0, 0))
    return pl.pallas_call(
        _xattn_body,
        grid=(b // gb, l // tm),
        in_specs=[qspec, mspec, mspec],
        out_specs=qspec,
        out_shape=jax.ShapeDtypeStruct((b, l, D_MODEL), BF16),
        compiler_params=_params("arbitrary", "arbitrary"),
        name="xattn",
    )(qx, mem_k, mem_v)


def _post2_body(x1_ref, o_ref, wxo_ref, gmlp_ref, wup_ref, wdown_ref, gfin_ref, y_ref):
    x2 = x1_ref[...] + _dot(o_ref[...], wxo_ref[...])
    hm = _rmsnorm(x2, gmlp_ref[...]).astype(BF16)
    acc = x2
    for c in range(D_FF // D_MODEL):
        sl = slice(c * D_MODEL, (c + 1) * D_MODEL)
        u = jnp.maximum(_dot(hm, wup_ref[:, sl]), 0.0)
        acc = acc + _dot((u * u).astype(BF16), wdown_ref[sl, :])
    y_ref[...] = _rmsnorm(acc, gfin_ref[...])


def _post2(x1, o, w_xo, g_mlp, w_up, w_down, g_final, tm):
    n, _ = x1.shape
    row = pl.BlockSpec((tm, D_MODEL), lambda i: (i, 0))
    return pl.pallas_call(
        _post2_body,
        grid=(n // tm,),
        in_specs=[row, row, _const_spec(w_xo.shape), _const_spec((1, D_MODEL)), _const_spec(w_up.shape),
                  _const_spec(w_down.shape), _const_spec((1, D_MODEL))],
        out_specs=row,
        out_shape=jax.ShapeDtypeStruct((n, D_MODEL), F32),
        compiler_params=_params("arbitrary"),
        name="post2",
    )(x1, o, w_xo, g_mlp, w_up, w_down, g_final)


def _rope_tables(pos):
    half = HEAD_DIM // 2
    inv = ROPE_THETA ** (-np.arange(half, dtype=np.float64) * 2.0 / HEAD_DIM)
    ang = np.asarray(pos, np.float64)[:, None] * inv[None, :]
    c, s = np.cos(ang), np.sin(ang)
    return (jnp.asarray(np.concatenate([c, c, c, c], axis=1), F32),
            jnp.asarray(np.concatenate([-s, s, -s, s], axis=1), F32))


def _pad_lanes(t):
    return jnp.pad(t, ((0, 0), (0, LANES - t.shape[1])))


def kernel(x_prompt, x_sample, cache_win_k, cache_win_v, state_conv, state_ssm, cache_mem_k, cache_mem_v,
           mem_prompt, g_mix, w_in, conv_w, conv_b, dt_bias, a_log, d_skip, g_ssm, w_out, g_xatt, g_mem,
           w_xq, w_mk, w_mv, w_xo, g_mlp, w_up, w_down, g_final):
    depth = w_in.shape[0]
    assert depth == 1, "kernel is written for the single-layer trunk of this problem"
    bp, s_len, _ = x_prompt.shape
    bs, t_new, _ = x_sample.shape
    lb = cache_win_k.shape[2]
    lw = min(MAX_WINDOW, s_len)
    n_proj = ATT_WIDTH * 3 + SSM_WIDTH + CONV_CH

    li = 0
    row = lambda t: t[li].reshape(1, -1)
    w_main = w_in[li, :, :n_proj].astype(BF16)
    w_dt = _pad_lanes(w_in[li, :, n_proj:]).astype(BF16)
    conv_w8 = jnp.pad(conv_w[li], ((0, 8 - CONV_WIDTH), (0, 0)))
    per_head_rows = lambda t: jnp.broadcast_to(t[li][:, None], (N_SSM_HEADS, SSD_CHUNK))
    dt_b = per_head_rows(dt_bias)
    a_lg = per_head_rows(a_log)
    d_sk = jnp.repeat(d_skip[li], HEAD_DIM).reshape(1, -1)
    w_o, w_q, w_k, w_v, w_x = (t[li].astype(BF16) for t in (w_out, w_xq, w_mk, w_mv, w_xo))
    w_u, w_d = w_up[li].astype(BF16), w_down[li].astype(BF16)
    g_fin = g_final.reshape(1, -1)

    cos_p, sin_p = _rope_tables(np.arange(s_len))
    q, k, v, k32, v32, z, xbc, dt = _in_proj(x_prompt, row(g_mix), w_main, w_dt, cos_p, sin_p, lw, ROW_TILE)
    att = _attn_prompt(q, k, v)
    y_ssm, ssm_p = _ssd(xbc, z, dt, jnp.zeros((bp, 8, CONV_CH), F32),
                        jnp.zeros((bp, SSM_WIDTH, SSM_STATE), F32), conv_w8, row(conv_b), dt_b, a_lg, d_sk,
                        row(g_ssm), s_len)
    mk, mv, mk_b, mv_b = _mem_kv(mem_prompt.reshape(bp * N_MEM, D_MODEL), row(g_mem), w_k, w_v, ROW_TILE)
    x1, qx = _post1(x_prompt, att, y_ssm, w_o, row(g_xatt), w_q, ROW_TILE)
    o = _xattn(qx, mk_b.reshape(bp, N_MEM, D_MODEL), mv_b.reshape(bp, N_MEM, D_MODEL), 1, ROW_TILE)
    y_prompt = _post2(x1.reshape(bp * s_len, D_MODEL), o.reshape(bp * s_len, D_MODEL), w_x, row(g_mlp),
                      w_u, w_d, g_fin, ROW_TILE)
    y_prompt = y_prompt.reshape(bp, s_len, D_MODEL)
    win_k_p = k32.reshape(1, bp, lw, N_ATT_HEADS, HEAD_DIM)
    win_v_p = v32.reshape(1, bp, lw, N_ATT_HEADS, HEAD_DIM)
    conv_p = xbc[:, s_len - (CONV_WIDTH - 1):][None]
    ssm_p = ssm_p.reshape(1, bp, N_SSM_HEADS, HEAD_DIM, SSM_STATE)
    mk_p = mk.reshape(1, bp, N_MEM, N_XATT_HEADS, XATT_HEAD_DIM)
    mv_p = mv.reshape(1, bp, N_MEM, N_XATT_HEADS, XATT_HEAD_DIM)

    r = SAMPLE_ROWS
    n_s = bs * r
    xs_pad = jnp.pad(x_sample, ((0, 0), (0, r - t_new), (0, 0))).reshape(1, n_s, D_MODEL)
    pos_s = PAST_LEN + (np.arange(n_s) % r)
    cos_s, sin_s = _rope_tables(pos_s)
    q, _, _, k32, v32, z, xbc, dt = _in_proj(xs_pad, row(g_mix), w_main, w_dt, cos_s, sin_s, n_s, ROW_TILE)
    to_minor = lambda c: jnp.transpose(c[li].reshape(bs, lb, ATT_WIDTH), (0, 2, 1))
    att, kt_s, vt_s = _attn_sample(q, k32, v32, to_minor(cache_win_k), to_minor(cache_win_v), r, t_new)
    per_seq = lambda t: t.reshape(bs, r, t.shape[-1])
    conv_prev8 = jnp.pad(state_conv[li], ((0, 0), (8 - (CONV_WIDTH - 1), 0), (0, 0)))
    xbc_s = per_seq(xbc)
    y_ssm, ssm_s = _ssd(xbc_s, per_seq(z), per_seq(dt), conv_prev8,
                        state_ssm[li].reshape(bs, SSM_WIDTH, SSM_STATE), conv_w8, row(conv_b), dt_b, a_lg, d_sk,
                        row(g_ssm), t_new)
    x1, qx = _post1(xs_pad, att, y_ssm.reshape(1, n_s, SSM_WIDTH), w_o, row(g_xatt), w_q, ROW_TILE)
    o = _xattn(qx.reshape(bs, r, D_MODEL), cache_mem_k[li].reshape(bs, N_MEM, D_MODEL),
               cache_mem_v[li].reshape(bs, N_MEM, D_MODEL), 4, r)
    y_s = _post2(x1.reshape(n_s, D_MODEL), o.reshape(n_s, D_MODEL), w_x, row(g_mlp), w_u, w_d, g_fin, ROW_TILE)
    y_sample = y_s.reshape(bs, r, D_MODEL)[:, :t_new]
    from_minor = lambda c: jnp.transpose(c, (0, 2, 1)).reshape(1, bs, lb, N_ATT_HEADS, HEAD_DIM)
    win_k_s, win_v_s = from_minor(kt_s), from_minor(vt_s)
    conv_s = jnp.concatenate([state_conv[li], xbc_s[:, :t_new]], axis=1)[:, -(CONV_WIDTH - 1):][None]
    ssm_s = ssm_s.reshape(1, bs, N_SSM_HEADS, HEAD_DIM, SSM_STATE)

    return (y_prompt, y_sample, win_k_p, win_v_p, conv_p, ssm_p, mk_p, mv_p,
            win_k_s, win_v_s, conv_s, ssm_s)
```

```python
import functools

import jax
import jax.numpy as jnp
import numpy as np
from jax import lax
from jax.experimental import pallas as pl
from jax.experimental.pallas import tpu as pltpu

F32 = jnp.float32
BF16 = jnp.bfloat16

D_MODEL = 1024
PAST_LEN = 8192
HEAD_DIM = 64
ATT_WIDTH = 512
N_ATT_HEADS = 8
DILATED_BRANCHES = ((128, 1), (512, 4), (2048, 16))
MAX_WINDOW = 2048
ROPE_THETA = 10000.0
SSM_WIDTH = 512
N_SSM_HEADS = 8
SSM_GROUPS = 2
SSM_STATE = 128
CONV_WIDTH = 4
SSD_CHUNK = 128
CONV_CH = 1024
N_MEM = 256
N_XATT_HEADS = 4
XATT_HEAD_DIM = 256
D_FF = 4096
EPS = 1e-6

LANES = 128
PAIR = 2 * HEAD_DIM
N_PAIRS = ATT_WIDTH // PAIR
BRANCH_BLOCK = 128
ATT_TILE = 2048
UNITS_PER_ITER = 4
ROW_TILE = 512
SAMPLE_ROWS = 16
VMEM_LIMIT = 56 * 1024 * 1024


def _params(*sem):
    return pltpu.CompilerParams(dimension_semantics=sem, vmem_limit_bytes=VMEM_LIMIT)


def _const_spec(shape):
    return pl.BlockSpec(shape, lambda *_: (0,) * len(shape), pipeline_mode=pl.Buffered(1))


def _rmsnorm(x, g):
    return x * lax.rsqrt(jnp.mean(x * x, axis=-1, keepdims=True) + EPS) * g


def _dot(a, b):
    return jnp.dot(a, b, preferred_element_type=F32)


def _dot_nt(a, b):
    return lax.dot_general(a, b, (((1,), (1,)), ((), ())), preferred_element_type=F32)


def _dot_tn(a, b):
    return lax.dot_general(a, b, (((0,), (0,)), ((), ())), preferred_element_type=F32)


def _in_proj_body(x_ref, g_ref, w_ref, wdt_ref, cos_ref, sin_ref,
                  q_ref, k_ref, v_ref, k32_ref, v32_ref, z_ref, xbc_ref, dt_ref):
    h = _rmsnorm(x_ref[0], g_ref[...]).astype(BF16)
    tm = h.shape[0]
    cos = jnp.concatenate([cos_ref[...]] * N_PAIRS, axis=1)
    sin = jnp.concatenate([sin_ref[...]] * N_PAIRS, axis=1)
    lane = lax.broadcasted_iota(jnp.int32, (tm, ATT_WIDTH), 1)
    first_half = (lane & (HEAD_DIM // 2)) == 0

    def rope(t):
        partner = jnp.where(first_half,
                            pltpu.roll(t, ATT_WIDTH - HEAD_DIM // 2, axis=1),
                            pltpu.roll(t, HEAD_DIM // 2, axis=1))
        return t * cos + partner * sin

    q = rope(_dot(h, w_ref[:, 0:512]))
    k = rope(_dot(h, w_ref[:, 512:1024]))
    v = _dot(h, w_ref[:, 1024:1536])
    for pr in range(N_PAIRS):
        sl = slice(pr * PAIR, (pr + 1) * PAIR)
        q_ref[0, pr] = q[:, sl]
        k_ref[0, pr] = k[:, sl]
        v_ref[0, pr] = v[:, sl]
    k32_ref[0] = k
    v32_ref[0] = v
    z_ref[0] = _dot(h, w_ref[:, 1536:2048])
    xbc_ref[0] = _dot(h, w_ref[:, 2048:3072])
    dt_ref[0] = _dot(h, wdt_ref[...])


def _in_proj(x, g, w_main, w_dt, cos, sin, window, tm):
    b, l, _ = x.shape
    nj = l // tm
    j0 = (l - window) // tm
    row = lambda bi, j: (bi, j, 0)
    slab = lambda bi, j: (bi, 0, j, 0)
    win = lambda bi, j: (bi, jnp.maximum(j - j0, 0), 0)
    slab_shape = jax.ShapeDtypeStruct((b, N_PAIRS, l, PAIR), F32)
    slab_spec = pl.BlockSpec((1, N_PAIRS, tm, PAIR), slab)
    outs = (
        slab_shape, slab_shape, slab_shape,
        jax.ShapeDtypeStruct((b, window, ATT_WIDTH), F32),
        jax.ShapeDtypeStruct((b, window, ATT_WIDTH), F32),
        jax.ShapeDtypeStruct((b, l, SSM_WIDTH), F32),
        jax.ShapeDtypeStruct((b, l, CONV_CH), F32),
        jax.ShapeDtypeStruct((b, l, LANES), F32),
    )
    return pl.pallas_call(
        _in_proj_body,
        grid=(b, nj),
        in_specs=[
            pl.BlockSpec((1, tm, D_MODEL), row),
            _const_spec((1, D_MODEL)),
            _const_spec(w_main.shape),
            _const_spec(w_dt.shape),
            pl.BlockSpec((tm, LANES), lambda bi, j: (j, 0)),
            pl.BlockSpec((tm, LANES), lambda bi, j: (j, 0)),
        ],
        out_specs=(
            slab_spec, slab_spec, slab_spec,
            pl.BlockSpec((1, tm, ATT_WIDTH), win),
            pl.BlockSpec((1, tm, ATT_WIDTH), win),
            pl.BlockSpec((1, tm, SSM_WIDTH), row),
            pl.BlockSpec((1, tm, CONV_CH), row),
            pl.BlockSpec((1, tm, LANES), row),
        ),
        out_shape=outs,
        compiler_params=_params("arbitrary", "arbitrary"),
        name="in_proj",
    )(x, g, w_main, w_dt, cos, sin)


def _attn_prompt_body(q_ref, kc_ref, kp_ref, vc_ref, vp_ref, o_ref, r0_ref, r1_ref, m0_ref, m1_ref):
    t = pl.program_id(2)
    n = BRANCH_BLOCK
    tile = ATT_TILE

    row = lax.broadcasted_iota(jnp.int32, (2 * n, 2 * n), 0) & (n - 1)
    col = lax.broadcasted_iota(jnp.int32, (2 * n, 2 * n), 1)
    bias_cur = jnp.where(col - n <= row, 0.0, -jnp.inf)
    bias = jnp.where(col < n, jnp.where(col >= row, 0.0, -jnp.inf), bias_cur)
    bias_first = jnp.where(col < n, -jnp.inf, bias_cur)
    low = lax.broadcasted_iota(jnp.int32, (n, PAIR), 1) < HEAD_DIM
    low2 = lax.broadcasted_iota(jnp.int32, (2 * n, PAIR), 1) < HEAD_DIM
    scale = HEAD_DIM ** -0.5

    def rows(start, stride):
        return pl.ds(start, n) if stride == 1 else pl.ds(start, n, stride=stride)

    def prev_block(cur_ref, prev_ref, q0, stride, where):
        if isinstance(where, str):
            if where == "cur":
                return cur_ref[rows(q0 - n * stride, stride), :]
            return prev_ref[rows(tile + q0 - n * stride, stride), :]
        return jnp.where(where, cur_ref[rows(jnp.maximum(q0 - n, 0), 1), :], prev_ref[tile - n:tile, :])

    def run_units(units, stride, merge, last=False):
        loaded = []
        for q0, where, _ in units:
            qv = q_ref[rows(q0, stride), :] * scale
            kk = jnp.concatenate([prev_block(kc_ref, kp_ref, q0, stride, where),
                                  kc_ref[rows(q0, stride), :]], axis=0)
            vv = jnp.concatenate([prev_block(vc_ref, vp_ref, q0, stride, where),
                                  vc_ref[rows(q0, stride), :]], axis=0)
            qm = jnp.concatenate([jnp.where(low, qv, 0.0), jnp.where(low, 0.0, qv)], axis=0).astype(BF16)
            loaded.append((qm, kk.astype(BF16),
                           jnp.where(low2, vv, 1.0).astype(BF16), jnp.where(low2, 1.0, vv).astype(BF16)))
        scores = []
        for (qm, kk, _, _), (_, _, has_prev) in zip(loaded, units):
            b = bias if has_prev is None else jnp.where(has_prev, bias, bias_first)
            scores.append(_dot_nt(qm, kk) + b)
        m_new, alpha = [], []
        for s, (q0, _, _) in zip(scores, units):
            m_row = jnp.max(s, axis=-1, keepdims=True)
            if merge:
                m_old = jnp.concatenate([m0_ref[rows(q0, stride), :], m1_ref[rows(q0, stride), :]], axis=0)
                m = jnp.maximum(m_old, m_row)
                alpha.append(jnp.exp(m_old - m))
            else:
                m = jnp.broadcast_to(m_row, (2 * n, PAIR))
            m_new.append(m)
        probs = [jnp.exp(s - jnp.concatenate([m, m], axis=1)).astype(BF16) for s, m in zip(scores, m_new)]
        for i, (q0, _, _) in enumerate(units):
            _, _, va0, va1 = loaded[i]
            acc0 = _dot(probs[i][0:n], va0)
            acc1 = _dot(probs[i][n:], va1)
            if merge:
                acc0 = acc0 + alpha[i][0:n] * r0_ref[rows(q0, stride), :]
                acc1 = acc1 + alpha[i][n:] * r1_ref[rows(q0, stride), :]
            if last:
                num = jnp.where(low, acc0, acc1)
                den = pltpu.roll(jnp.where(low, acc1, acc0), HEAD_DIM, axis=1)
                o_ref[rows(q0, stride), :] = num / den
            else:
                r0_ref[rows(q0, stride), :] = acc0
                r1_ref[rows(q0, stride), :] = acc1
                m0_ref[rows(q0, stride), :] = m_new[i][0:n]
                m1_ref[rows(q0, stride), :] = m_new[i][n:]

    g_units = UNITS_PER_ITER
    not_first_tile = t > 0

    def branch_d16(it, carry):
        units = [(it * g_units + g, "prev", not_first_tile) for g in range(g_units)]
        run_units(units, 16, merge=False)
        return carry

    lax.fori_loop(0, 16 // g_units, branch_d16, 0)

    blocks_d4 = tile // (4 * n)
    streams_d4 = max(1, g_units // blocks_d4)

    def branch_d4(it, carry):
        units = [(i * (4 * n) + it * streams_d4 + st, "prev" if i == 0 else "cur",
                  not_first_tile if i == 0 else None)
                 for st in range(streams_d4) for i in range(blocks_d4)]
        run_units(units, 4, merge=True)
        return carry

    lax.fori_loop(0, 4 // streams_d4, branch_d4, 0)

    def branch_d1(it, carry):
        base = pl.multiple_of(it * (g_units * n), g_units * n)
        units = [(base, it > 0, jnp.logical_or(not_first_tile, it > 0))]
        units += [(base + g * n, "cur", None) for g in range(1, g_units)]
        run_units(units, 1, merge=True, last=True)
        return carry

    lax.fori_loop(0, tile // (g_units * n), branch_d1, 0)


def _attn_prompt(q, k, v):
    b, npair, s, w = q.shape
    assert [d for _, d in DILATED_BRANCHES] == [1, 4, 16] and s % ATT_TILE == 0
    tile = ATT_TILE
    cur = pl.BlockSpec((None, None, tile, w), lambda bi, p, t: (bi, p, t, 0))
    prev = pl.BlockSpec((None, None, tile, w), lambda bi, p, t: (bi, p, jnp.maximum(t - 1, 0), 0))
    return pl.pallas_call(
        _attn_prompt_body,
        grid=(b, npair, s // tile),
        in_specs=[cur, cur, prev, cur, prev],
        out_specs=cur,
        out_shape=jax.ShapeDtypeStruct((b, npair, s, w), F32),
        scratch_shapes=[pltpu.VMEM((tile, w), F32)] * 4,
        compiler_params=_params("arbitrary", "arbitrary", "arbitrary"),
        name="attn_prompt",
    )(q, k, k, v, v)


def _branch_count(dist):
    cnt = jnp.zeros(dist.shape, F32)
    for window, dil in DILATED_BRANCHES:
        hit = (dist >= 0) & (dist <= window) & ((dist & (dil - 1)) == 0)
        cnt = cnt + hit.astype(F32)
    return cnt


def _attn_sample_body(q_ref, kn_ref, vn_ref, kt_ref, vt_ref, att_ref, kto_ref, vto_ref, *, n_new):
    lb = kt_ref.shape[1]
    rows = q_ref.shape[1]
    k_new = kn_ref[...]
    v_new = vn_ref[...]

    tail_lane = lax.broadcasted_iota(jnp.int32, (64, LANES), 1) >= LANES - n_new
    for new, src_ref, dst_ref in ((k_new, kt_ref, kto_ref), (v_new, vt_ref, vto_ref)):
        last8 = pltpu.roll(new[0:8], 8 - n_new, axis=0)
        padded = jnp.concatenate([jnp.zeros((LANES - 8, ATT_WIDTH), F32), last8], axis=0)
        new_t = jnp.concatenate([padded[:, p * LANES:(p + 1) * LANES].T for p in range(N_PAIRS)], axis=0)
        for c in range(ATT_WIDTH // 64):
            shifted = pltpu.roll(src_ref[c * 64:(c + 1) * 64, :], lb - n_new, axis=1)
            dst_ref[c * 64:(c + 1) * 64, 0:lb - LANES] = shifted[:, 0:lb - LANES]
            dst_ref[c * 64:(c + 1) * 64, lb - LANES:lb] = jnp.where(
                tail_lane, new_t[c * 64:(c + 1) * 64], shifted[:, lb - LANES:lb])

    t_c = lax.broadcasted_iota(jnp.int32, (rows, lb), 0)
    i_c = lax.broadcasted_iota(jnp.int32, (rows, lb), 1)
    cnt_c = _branch_count(lb + t_c - i_c)
    t_n = lax.broadcasted_iota(jnp.int32, (rows, rows), 0)
    i_n = lax.broadcasted_iota(jnp.int32, (rows, rows), 1)
    cnt_n = jnp.where(i_n < n_new, _branch_count(t_n - i_n), 0.0)
    low = lax.broadcasted_iota(jnp.int32, (rows, PAIR), 1) < HEAD_DIM
    scale = HEAD_DIM ** -0.5
    for pr in range(N_PAIRS):
        sl = slice(pr * PAIR, (pr + 1) * PAIR)
        qp = q_ref[pr]
        ktp = kt_ref[sl, :].astype(BF16)
        vtp = vt_ref[sl, :].astype(BF16)
        knp = k_new[:, sl].astype(BF16)
        vnp = v_new[:, sl].astype(BF16)
        o_h = []
        for hh in range(2):
            qm = jnp.where(low if hh == 0 else jnp.logical_not(low), qp, 0.0).astype(BF16)
            s_c = jnp.where(cnt_c > 0, _dot(qm, ktp) * scale, -jnp.inf)
            s_n = jnp.where(cnt_n > 0, _dot_nt(qm, knp) * scale, -jnp.inf)
            m = jnp.maximum(jnp.max(s_c, axis=-1, keepdims=True), jnp.max(s_n, axis=-1, keepdims=True))
            p_c = cnt_c * jnp.exp(s_c - m)
            p_n = cnt_n * jnp.exp(s_n - m)
            den = jnp.sum(p_c, axis=-1, keepdims=True) + jnp.sum(p_n, axis=-1, keepdims=True)
            pv = _dot_nt(p_c.astype(BF16), vtp) + _dot(p_n.astype(BF16), vnp)
            o_h.append(pv / den)
        att_ref[pr] = jnp.where(low, o_h[0], o_h[1])


def _attn_sample(q, k_new, v_new, kt_cache, vt_cache, rows, n_new):
    b, w, lb = kt_cache.shape
    assert n_new <= 8 and lb % LANES == 0
    slab = pl.BlockSpec((None, N_PAIRS, rows, PAIR), lambda bi: (0, 0, bi, 0))
    new = pl.BlockSpec((None, rows, w), lambda bi: (0, bi, 0))
    cache = pl.BlockSpec((None, w, lb), lambda bi: (bi, 0, 0))
    return pl.pallas_call(
        functools.partial(_attn_sample_body, n_new=n_new),
        grid=(b,),
        in_specs=[slab, new, new, cache, cache],
        out_specs=(slab, cache, cache),
        out_shape=(jax.ShapeDtypeStruct(q.shape, F32),
                   jax.ShapeDtypeStruct((b, w, lb), F32),
                   jax.ShapeDtypeStruct((b, w, lb), F32)),
        compiler_params=_params("arbitrary"),
        name="attn_sample",
    )(q, k_new, v_new, kt_cache, vt_cache)


def _ssd_body(xbc_ref, z_ref, dt_ref, cprev_ref, h0_ref, cw_ref, cb_ref, dtb_ref, alog_ref, dskip_ref, g_ref,
              y_ref, hout_ref, h_ref, xprev_ref, *, q, chunks, n_valid):
    @pl.when(pl.program_id(1) == 0)
    def _():
        h_ref[...] = h0_ref[0]
        xprev_ref[...] = cprev_ref[0]

    def one_chunk(ck, carry):
        _ssd_chunk(pl.program_id(1) * chunks + ck, pl.multiple_of(ck * q, q),
                   xbc_ref, z_ref, dt_ref, cw_ref, cb_ref, dtb_ref, alog_ref, dskip_ref, g_ref, y_ref,
                   h_ref, xprev_ref, q=q, n_valid=n_valid)
        return carry

    lax.fori_loop(0, chunks, one_chunk, 0)
    hout_ref[0] = h_ref[...]


def _silu(x):
    half = 0.5 * x
    return half + half * jnp.tanh(half)


def _ssd_chunk(chunk, r0, xbc_ref, z_ref, dt_ref, cw_ref, cb_ref, dtb_ref, alog_ref, dskip_ref, g_ref, y_ref,
               h_ref, xprev_ref, *, q, n_valid):
    def chunk_rows(ref):
        return ref[0, pl.ds(r0, q), :]

    def to_columns(t8):
        return jnp.concatenate([t8, jnp.zeros((LANES - 8, q), F32)], axis=0).T

    cur = chunk_rows(xbc_ref)
    prev8 = xprev_ref[...]
    i8 = lax.broadcasted_iota(jnp.int32, (8, CONV_CH), 0)
    acc = cb_ref[...] + cur * cw_ref[CONV_WIDTH - 1:CONV_WIDTH, :]
    for sh in range(1, CONV_WIDTH):
        rolled = pltpu.roll(cur, sh, axis=0)
        head = jnp.where(i8 < sh, pltpu.roll(prev8, sh, axis=0), rolled[0:8])
        shifted = jnp.concatenate([head, rolled[8:]], axis=0)
        acc = acc + shifted * cw_ref[CONV_WIDTH - 1 - sh:CONV_WIDTH - sh, :]
    xprev_ref[...] = cur[q - 8:q]
    conv = _silu(acc)
    xs = conv[:, 0:SSM_WIDTH]

    tok = lax.broadcasted_iota(jnp.int32, (N_SSM_HEADS, q), 1)
    dt_in = chunk_rows(dt_ref).T[0:N_SSM_HEADS, :] + dtb_ref[:, 0:q]
    dtv_t = jnp.maximum(dt_in, 0.0) + jnp.log1p(jnp.exp(-jnp.abs(dt_in)))
    dtv_t = jnp.where(chunk * q + tok < n_valid, dtv_t, 0.0)
    a_t = dtv_t * (-jnp.exp(alog_ref[:, 0:q]))
    row_l = lax.broadcasted_iota(jnp.int32, (q, q), 0)
    lane_l = lax.broadcasted_iota(jnp.int32, (q, q), 1)
    tri = row_l >= lane_l
    a_cs_t = jnp.dot(a_t, (row_l <= lane_l).astype(F32), preferred_element_type=F32,
                     precision=lax.Precision.HIGHEST)
    a_cs = to_columns(a_cs_t)
    dtv = to_columns(dtv_t)
    low = lax.broadcasted_iota(jnp.int32, (q, PAIR), 1) < HEAD_DIM

    def per_pair(t, pr):
        return jnp.where(low, t[:, 2 * pr:2 * pr + 1], t[:, 2 * pr + 1:2 * pr + 2])

    n_pairs = N_SSM_HEADS // 2
    pairs_per_group = n_pairs // SSM_GROUPS
    decay_in = jnp.exp(a_cs)
    decay_out = jnp.exp(a_cs[q - 1:q, :] - a_cs)
    total = jnp.exp(a_cs_t[:, q - 1:q])
    bmb, cmb, cb = [], [], []
    for grp in range(SSM_GROUPS):
        b0 = SSM_WIDTH + grp * SSM_STATE
        c0 = SSM_WIDTH + (SSM_GROUPS + grp) * SSM_STATE
        bmb.append(conv[:, b0:b0 + SSM_STATE].astype(BF16))
        cmb.append(conv[:, c0:c0 + SSM_STATE].astype(BF16))
        cb.append(_dot_nt(cmb[grp], bmb[grp]))
    xs_p = [xs[:, pr * PAIR:(pr + 1) * PAIR] for pr in range(n_pairs)]
    xdt = [xs_p[pr] * per_pair(dtv, pr) for pr in range(n_pairs)]
    xdt_b = [t.astype(BF16) for t in xdt]
    scores = [(cb[hd // (N_SSM_HEADS // SSM_GROUPS)]
               * jnp.exp(jnp.where(tri, a_cs[:, hd:hd + 1] - a_cs_t[hd:hd + 1, :], -jnp.inf))).astype(BF16)
              for hd in range(N_SSM_HEADS)]
    y_diag = [jnp.where(low, _dot(scores[2 * pr], xdt_b[pr]), _dot(scores[2 * pr + 1], xdt_b[pr]))
              for pr in range(n_pairs)]
    states = [_dot_tn((xdt[pr] * per_pair(decay_out, pr)).astype(BF16), bmb[pr // pairs_per_group])
              for pr in range(n_pairs)]
    h_old = [h_ref[pr * PAIR:(pr + 1) * PAIR, :] for pr in range(n_pairs)]
    y_off = [_dot_nt(cmb[pr // pairs_per_group], h_old[pr].astype(BF16)) * per_pair(decay_in, pr)
             for pr in range(n_pairs)]
    ys = []
    for pr in range(n_pairs):
        chunk_decay = jnp.concatenate([jnp.broadcast_to(total[2 * pr:2 * pr + 1], (HEAD_DIM, SSM_STATE)),
                                       jnp.broadcast_to(total[2 * pr + 1:2 * pr + 2], (HEAD_DIM, SSM_STATE))],
                                      axis=0)
        h_ref[pr * PAIR:(pr + 1) * PAIR, :] = h_old[pr] * chunk_decay + states[pr]
        ys.append(y_diag[pr] + y_off[pr] + dskip_ref[:, pr * PAIR:(pr + 1) * PAIR] * xs_p[pr])
    y = jnp.concatenate(ys, axis=1)

    u = y * _silu(chunk_rows(z_ref))
    gw = SSM_WIDTH // SSM_GROUPS
    normed = []
    for grp in range(SSM_GROUPS):
        ug = u[:, grp * gw:(grp + 1) * gw]
        normed.append(ug * lax.rsqrt(jnp.mean(ug * ug, axis=-1, keepdims=True) + EPS))
    out = jnp.concatenate(normed, axis=1) * g_ref[...]
    y_ref[0, pl.ds(r0, q), :] = out


def _ssd(xbc, z, dt, conv_prev8, h0, conv_w8, conv_b, dt_bias, a_log, d_skip, g_ssm, n_valid):
    b, l, _ = xbc.shape
    q = min(SSD_CHUNK, l)
    chunks = 4 if l % (4 * q) == 0 else 1
    rows_in = q * chunks
    nc = l // rows_in
    row = lambda bi, c: (bi, c, 0)
    per_b = lambda bi, c: (bi, 0, 0)
    return pl.pallas_call(
        functools.partial(_ssd_body, q=q, chunks=chunks, n_valid=n_valid),
        grid=(b, nc),
        in_specs=[
            pl.BlockSpec((1, rows_in, CONV_CH), row),
            pl.BlockSpec((1, rows_in, SSM_WIDTH), row),
            pl.BlockSpec((1, rows_in, LANES), row),
            pl.BlockSpec((1, 8, CONV_CH), per_b),
            pl.BlockSpec((1, SSM_WIDTH, SSM_STATE), per_b),
            _const_spec((8, CONV_CH)),
            _const_spec((1, CONV_CH)),
            _const_spec((N_SSM_HEADS, SSD_CHUNK)),
            _const_spec((N_SSM_HEADS, SSD_CHUNK)),
            _const_spec((1, SSM_WIDTH)),
            _const_spec((1, SSM_WIDTH)),
        ],
        out_specs=(pl.BlockSpec((1, rows_in, SSM_WIDTH), row),
                   pl.BlockSpec((1, SSM_WIDTH, SSM_STATE), per_b)),
        out_shape=(jax.ShapeDtypeStruct((b, l, SSM_WIDTH), F32),
                   jax.ShapeDtypeStruct((b, SSM_WIDTH, SSM_STATE), F32)),
        scratch_shapes=[pltpu.VMEM((SSM_WIDTH, SSM_STATE), F32), pltpu.VMEM((8, CONV_CH), F32)],
        compiler_params=_params("arbitrary", "arbitrary"),
        name="ssd",
    )(xbc, z, dt, conv_prev8, h0, conv_w8, conv_b, dt_bias, a_log, d_skip, g_ssm)


def _post1_body(x_ref, att_ref, y_ref, wout_ref, g_ref, wxq_ref, x1_ref, qx_ref):
    mixed = jnp.concatenate([att_ref[0, pr] for pr in range(N_PAIRS)] + [y_ref[0]], axis=1).astype(BF16)
    x1 = x_ref[0] + _dot(mixed, wout_ref[...])
    x1_ref[0] = x1
    qx_ref[0] = _dot(_rmsnorm(x1, g_ref[...]).astype(BF16), wxq_ref[...]).astype(BF16)


def _post1(x, att, y, w_out, g_xatt, w_xq, tm):
    b, l, _ = x.shape
    row = lambda bi, j: (bi, j, 0)
    full = pl.BlockSpec((1, tm, D_MODEL), row)
    return pl.pallas_call(
        _post1_body,
        grid=(b, l // tm),
        in_specs=[full,
                  pl.BlockSpec((1, N_PAIRS, tm, PAIR), lambda bi, j: (bi, 0, j, 0)),
                  pl.BlockSpec((1, tm, SSM_WIDTH), row),
                  _const_spec(w_out.shape), _const_spec((1, D_MODEL)), _const_spec(w_xq.shape)],
        out_specs=(full, full),
        out_shape=(jax.ShapeDtypeStruct((b, l, D_MODEL), F32), jax.ShapeDtypeStruct((b, l, D_MODEL), BF16)),
        compiler_params=_params("arbitrary", "arbitrary"),
        name="post1",
    )(x, att, y, w_out, g_xatt, w_xq)


def _mem_kv_body(m_ref, g_ref, wk_ref, wv_ref, k_ref, v_ref, kb_ref, vb_ref):
    h = _rmsnorm(m_ref[...], g_ref[...]).astype(BF16)
    k = _dot(h, wk_ref[...])
    v = _dot(h, wv_ref[...])
    k_ref[...] = k
    v_ref[...] = v
    kb_ref[...] = k.astype(BF16)
    vb_ref[...] = v.astype(BF16)


def _mem_kv(mem, g_mem, w_mk, w_mv, tm):
    n, _ = mem.shape
    row = pl.BlockSpec((tm, D_MODEL), lambda i: (i, 0))
    return pl.pallas_call(
        _mem_kv_body,
        grid=(n // tm,),
        in_specs=[row, _const_spec((1, D_MODEL)), _const_spec(w_mk.shape), _const_spec(w_mv.shape)],
        out_specs=(row, row, row, row),
        out_shape=(jax.ShapeDtypeStruct((n, D_MODEL), F32), jax.ShapeDtypeStruct((n, D_MODEL), F32),
                   jax.ShapeDtypeStruct((n, D_MODEL), BF16), jax.ShapeDtypeStruct((n, D_MODEL), BF16)),
        compiler_params=_params("arbitrary"),
        name="mem_kv",
    )(mem, g_mem, w_mk, w_mv)


def _xattn_body(q_ref, mk_ref, mv_ref, o_ref):
    scale = XATT_HEAD_DIM ** -0.5
    for gi in range(q_ref.shape[0]):
        for hd in range(N_XATT_HEADS):
            sl = slice(hd * XATT_HEAD_DIM, (hd + 1) * XATT_HEAD_DIM)
            s = _dot_nt(q_ref[gi, :, sl], mk_ref[gi, :, sl].astype(BF16)) * scale
            p = jnp.exp(s - jnp.max(s, axis=-1, keepdims=True))
            den = jnp.sum(p, axis=-1, keepdims=True)
            o = _dot(p.astype(BF16), mv_ref[gi, :, sl].astype(BF16)) / den
            o_ref[gi, :, sl] = o.astype(BF16)


def _xattn(qx, mem_k, mem_v, gb, tm):
    b, l, _ = qx.shape
    qspec = pl.BlockSpec((gb, tm, D_MODEL), lambda bi, j: (bi, j, 0))
    mspec = pl.BlockSpec((gb, N_MEM, D_MODEL), lambda bi, j: (bi, 0, 0))
    return pl.pallas_call(
        _xattn_body,
        grid=(b // gb, l // tm),
        in_specs=[qspec, mspec, mspec],
        out_specs=qspec,
        out_shape=jax.ShapeDtypeStruct((b, l, D_MODEL), BF16),
        compiler_params=_params("arbitrary", "arbitrary"),
        name="xattn",
    )(qx, mem_k, mem_v)


def _post2_body(x1_ref, o_ref, wxo_ref, gmlp_ref, wup_ref, wdown_ref, gfin_ref, y_ref):
    x2 = x1_ref[...] + _dot(o_ref[...], wxo_ref[...])
    hm = _rmsnorm(x2, gmlp_ref[...]).astype(BF16)
    acc = x2
    for c in range(D_FF // D_MODEL):
        sl = slice(c * D_MODEL, (c + 1) * D_MODEL)
        u = jnp.maximum(_dot(hm, wup_ref[:, sl]), 0.0)
        acc = acc + _dot((u * u).astype(BF16), wdown_ref[sl, :])
    y_ref[...] = _rmsnorm(acc, gfin_ref[...])


def _post2(x1, o, w_xo, g_mlp, w_up, w_down, g_final, tm):
    n, _ = x1.shape
    row = pl.BlockSpec((tm, D_MODEL), lambda i: (i, 0))
    return pl.pallas_call(
        _post2_body,
        grid=(n // tm,),
        in_specs=[row, row, _const_spec(w_xo.shape), _const_spec((1, D_MODEL)), _const_spec(w_up.shape),
                  _const_spec(w_down.shape), _const_spec((1, D_MODEL))],
        out_specs=row,
        out_shape=jax.ShapeDtypeStruct((n, D_MODEL), F32),
        compiler_params=_params("arbitrary"),
        name="post2",
    )(x1, o, w_xo, g_mlp, w_up, w_down, g_final)


def _rope_tables(pos):
    half = HEAD_DIM // 2
    inv = ROPE_THETA ** (-np.arange(half, dtype=np.float64) * 2.0 / HEAD_DIM)
    ang = np.asarray(pos, np.float64)[:, None] * inv[None, :]
    c, s = np.cos(ang), np.sin(ang)
    return (jnp.asarray(np.concatenate([c, c, c, c], axis=1), F32),
            jnp.asarray(np.concatenate([-s, s, -s, s], axis=1), F32))


def _pad_lanes(t):
    return jnp.pad(t, ((0, 0), (0, LANES - t.shape[1])))


def kernel(x_prompt, x_sample, cache_win_k, cache_win_v, state_conv, state_ssm, cache_mem_k, cache_mem_v,
           mem_prompt, g_mix, w_in, conv_w, conv_b, dt_bias, a_log, d_skip, g_ssm, w_out, g_xatt, g_mem,
           w_xq, w_mk, w_mv, w_xo, g_mlp, w_up, w_down, g_final):
    depth = w_in.shape[0]
    assert depth == 1, "kernel is written for the single-layer trunk of this problem"
    bp, s_len, _ = x_prompt.shape
    bs, t_new, _ = x_sample.shape
    lb = cache_win_k.shape[2]
    lw = min(MAX_WINDOW, s_len)
    n_proj = ATT_WIDTH * 3 + SSM_WIDTH + CONV_CH

    li = 0
    row = lambda t: t[li].reshape(1, -1)
    w_main = w_in[li, :, :n_proj].astype(BF16)
    w_dt = _pad_lanes(w_in[li, :, n_proj:]).astype(BF16)
    conv_w8 = jnp.pad(conv_w[li], ((0, 8 - CONV_WIDTH), (0, 0)))
    per_head_rows = lambda t: jnp.broadcast_to(t[li][:, None], (N_SSM_HEADS, SSD_CHUNK))
    dt_b = per_head_rows(dt_bias)
    a_lg = per_head_rows(a_log)
    d_sk = jnp.repeat(d_skip[li], HEAD_DIM).reshape(1, -1)
    w_o, w_q, w_k, w_v, w_x = (t[li].astype(BF16) for t in (w_out, w_xq, w_mk, w_mv, w_xo))
    w_u, w_d = w_up[li].astype(BF16), w_down[li].astype(BF16)
    g_fin = g_final.reshape(1, -1)

    cos_p, sin_p = _rope_tables(np.arange(s_len))
    q, k, v, k32, v32, z, xbc, dt = _in_proj(x_prompt, row(g_mix), w_main, w_dt, cos_p, sin_p, lw, ROW_TILE)
    att = _attn_prompt(q, k, v)
    y_ssm, ssm_p = _ssd(xbc, z, dt, jnp.zeros((bp, 8, CONV_CH), F32),
                        jnp.zeros((bp, SSM_WIDTH, SSM_STATE), F32), conv_w8, row(conv_b), dt_b, a_lg, d_sk,
                        row(g_ssm), s_len)
    mk, mv, mk_b, mv_b = _mem_kv(mem_prompt.reshape(bp * N_MEM, D_MODEL), row(g_mem), w_k, w_v, ROW_TILE)
    x1, qx = _post1(x_prompt, att, y_ssm, w_o, row(g_xatt), w_q, ROW_TILE)
    o = _xattn(qx, mk_b.reshape(bp, N_MEM, D_MODEL), mv_b.reshape(bp, N_MEM, D_MODEL), 1, ROW_TILE)
    y_prompt = _post2(x1.reshape(bp * s_len, D_MODEL), o.reshape(bp * s_len, D_MODEL), w_x, row(g_mlp),
                      w_u, w_d, g_fin, ROW_TILE)
    y_prompt = y_prompt.reshape(bp, s_len, D_MODEL)
    win_k_p = k32.reshape(1, bp, lw, N_ATT_HEADS, HEAD_DIM)
    win_v_p = v32.reshape(1, bp, lw, N_ATT_HEADS, HEAD_DIM)
    conv_p = xbc[:, s_len - (CONV_WIDTH - 1):][None]
    ssm_p = ssm_p.reshape(1, bp, N_SSM_HEADS, HEAD_DIM, SSM_STATE)
    mk_p = mk.reshape(1, bp, N_MEM, N_XATT_HEADS, XATT_HEAD_DIM)
    mv_p = mv.reshape(1, bp, N_MEM, N_XATT_HEADS, XATT_HEAD_DIM)

    r = SAMPLE_ROWS
    n_s = bs * r
    xs_pad = jnp.pad(x_sample, ((0, 0), (0, r - t_new), (0, 0))).reshape(1, n_s, D_MODEL)
    pos_s = PAST_LEN + (np.arange(n_s) % r)
    cos_s, sin_s = _rope_tables(pos_s)
    q, _, _, k32, v32, z, xbc, dt = _in_proj(xs_pad, row(g_mix), w_main, w_dt, cos_s, sin_s, n_s, ROW_TILE)
    to_minor = lambda c: jnp.transpose(c[li].reshape(bs, lb, ATT_WIDTH), (0, 2, 1))
    att, kt_s, vt_s = _attn_sample(q, k32, v32, to_minor(cache_win_k), to_minor(cache_win_v), r, t_new)
    per_seq = lambda t: t.reshape(bs, r, t.shape[-1])
    conv_prev8 = jnp.pad(state_conv[li], ((0, 0), (8 - (CONV_WIDTH - 1), 0), (0, 0)))
    xbc_s = per_seq(xbc)
    y_ssm, ssm_s = _ssd(xbc_s, per_seq(z), per_seq(dt), conv_prev8,
                        state_ssm[li].reshape(bs, SSM_WIDTH, SSM_STATE), conv_w8, row(conv_b), dt_b, a_lg, d_sk,
                        row(g_ssm), t_new)
    x1, qx = _post1(xs_pad, att, y_ssm.reshape(1, n_s, SSM_WIDTH), w_o, row(g_xatt), w_q, ROW_TILE)
    o = _xattn(qx.reshape(bs, r, D_MODEL), cache_mem_k[li].reshape(bs, N_MEM, D_MODEL),
               cache_mem_v[li].reshape(bs, N_MEM, D_MODEL), 4, r)
    y_s = _post2(x1.reshape(n_s, D_MODEL), o.reshape(n_s, D_MODEL), w_x, row(g_mlp), w_u, w_d, g_fin, ROW_TILE)
    y_sample = y_s.reshape(bs, r, D_MODEL)[:, :t_new]
    from_minor = lambda c: jnp.transpose(c, (0, 2, 1)).reshape(1, bs, lb, N_ATT_HEADS, HEAD_DIM)
    win_k_s, win_v_s = from_minor(kt_s), from_minor(vt_s)
    conv_s = jnp.concatenate([state_conv[li], xbc_s[:, :t_new]], axis=1)[:, -(CONV_WIDTH - 1):][None]
    ssm_s = ssm_s.reshape(1, bs, N_SSM_HEADS, HEAD_DIM, SSM_STATE)

    return (y_prompt, y_sample, win_k_p, win_v_p, conv_p, ssm_p, mk_p, mv_p,
            win_k_s, win_v_s, conv_s, ssm_s)
```

```python
import functools

import jax
import jax.numpy as jnp
import numpy as np
from jax import lax
from jax.experimental import pallas as pl
from jax.experimental.pallas import tpu as pltpu

F32 = jnp.float32
BF16 = jnp.bfloat16

D_MODEL = 1024
PAST_LEN = 8192
HEAD_DIM = 64
ATT_WIDTH = 512
N_ATT_HEADS = 8
DILATED_BRANCHES = ((128, 1), (512, 4), (2048, 16))
MAX_WINDOW = 2048
ROPE_THETA = 10000.0
SSM_WIDTH = 512
N_SSM_HEADS = 8
SSM_GROUPS = 2
SSM_STATE = 128
CONV_WIDTH = 4
SSD_CHUNK = 128
CONV_CH = 1024
N_MEM = 256
N_XATT_HEADS = 4
XATT_HEAD_DIM = 256
D_FF = 4096
EPS = 1e-6

LANES = 128
PAIR = 2 * HEAD_DIM
N_PAIRS = ATT_WIDTH // PAIR
BRANCH_BLOCK = 128
ATT_TILE = 2048
UNITS_PER_ITER = 4
ROW_TILE = 512
SAMPLE_ROWS = 16
VMEM_LIMIT = 56 * 1024 * 1024


def _params(*sem):
    return pltpu.CompilerParams(dimension_semantics=sem, vmem_limit_bytes=VMEM_LIMIT)


def _const_spec(shape):
    return pl.BlockSpec(shape, lambda *_: (0,) * len(shape), pipeline_mode=pl.Buffered(1))


def _rmsnorm(x, g):
    return x * lax.rsqrt(jnp.mean(x * x, axis=-1, keepdims=True) + EPS) * g


def _dot(a, b):
    return jnp.dot(a, b, preferred_element_type=F32)


def _dot_nt(a, b):
    return lax.dot_general(a, b, (((1,), (1,)), ((), ())), preferred_element_type=F32)


def _dot_tn(a, b):
    return lax.dot_general(a, b, (((0,), (0,)), ((), ())), preferred_element_type=F32)


def _in_proj_body(x_ref, g_ref, w_ref, wdt_ref, cos_ref, sin_ref,
                  q_ref, k_ref, v_ref, k32_ref, v32_ref, z_ref, xbc_ref, dt_ref):
    h = _rmsnorm(x_ref[0], g_ref[...]).astype(BF16)
    tm = h.shape[0]
    cos = jnp.concatenate([cos_ref[...]] * N_PAIRS, axis=1)
    sin = jnp.concatenate([sin_ref[...]] * N_PAIRS, axis=1)
    lane = lax.broadcasted_iota(jnp.int32, (tm, ATT_WIDTH), 1)
    first_half = (lane & (HEAD_DIM // 2)) == 0

    def rope(t):
        partner = jnp.where(first_half,
                            pltpu.roll(t, ATT_WIDTH - HEAD_DIM // 2, axis=1),
                            pltpu.roll(t, HEAD_DIM // 2, axis=1))
        return t * cos + partner * sin

    q = rope(_dot(h, w_ref[:, 0:512]))
    k = rope(_dot(h, w_ref[:, 512:1024]))
    v = _dot(h, w_ref[:, 1024:1536])
    for pr in range(N_PAIRS):
        sl = slice(pr * PAIR, (pr + 1) * PAIR)
        q_ref[0, pr] = q[:, sl]
        k_ref[0, pr] = k[:, sl]
        v_ref[0, pr] = v[:, sl]
    k32_ref[0] = k
    v32_ref[0] = v
    z_ref[0] = _dot(h, w_ref[:, 1536:2048])
    xbc_ref[0] = _dot(h, w_ref[:, 2048:3072])
    dt_ref[0] = _dot(h, wdt_ref[...])


def _in_proj(x, g, w_main, w_dt, cos, sin, window, tm):
    b, l, _ = x.shape
    nj = l // tm
    j0 = (l - window) // tm
    row = lambda bi, j: (bi, j, 0)
    slab = lambda bi, j: (bi, 0, j, 0)
    win = lambda bi, j: (bi, jnp.maximum(j - j0, 0), 0)
    slab_shape = jax.ShapeDtypeStruct((b, N_PAIRS, l, PAIR), F32)
    slab_spec = pl.BlockSpec((1, N_PAIRS, tm, PAIR), slab)
    outs = (
        slab_shape, slab_shape, slab_shape,
        jax.ShapeDtypeStruct((b, window, ATT_WIDTH), F32),
        jax.ShapeDtypeStruct((b, window, ATT_WIDTH), F32),
        jax.ShapeDtypeStruct((b, l, SSM_WIDTH), F32),
        jax.ShapeDtypeStruct((b, l, CONV_CH), F32),
        jax.ShapeDtypeStruct((b, l, LANES), F32),
    )
    return pl.pallas_call(
        _in_proj_body,
        grid=(b, nj),
        in_specs=[
            pl.BlockSpec((1, tm, D_MODEL), row),
            _const_spec((1, D_MODEL)),
            _const_spec(w_main.shape),
            _const_spec(w_dt.shape),
            pl.BlockSpec((tm, LANES), lambda bi, j: (j, 0)),
            pl.BlockSpec((tm, LANES), lambda bi, j: (j, 0)),
        ],
        out_specs=(
            slab_spec, slab_spec, slab_spec,
            pl.BlockSpec((1, tm, ATT_WIDTH), win),
            pl.BlockSpec((1, tm, ATT_WIDTH), win),
            pl.BlockSpec((1, tm, SSM_WIDTH), row),
            pl.BlockSpec((1, tm, CONV_CH), row),
            pl.BlockSpec((1, tm, LANES), row),
        ),
        out_shape=outs,
        compiler_params=_params("arbitrary", "arbitrary"),
        name="in_proj",
    )(x, g, w_main, w_dt, cos, sin)


def _attn_prompt_body(q_ref, kc_ref, kp_ref, vc_ref, vp_ref, o_ref, r0_ref, r1_ref, m0_ref, m1_ref):
    t = pl.program_id(2)
    n = BRANCH_BLOCK
    tile = ATT_TILE

    row = lax.broadcasted_iota(jnp.int32, (2 * n, 2 * n), 0) & (n - 1)
    col = lax.broadcasted_iota(jnp.int32, (2 * n, 2 * n), 1)
    bias_cur = jnp.where(col - n <= row, 0.0, -jnp.inf)
    bias = jnp.where(col < n, jnp.where(col >= row, 0.0, -jnp.inf), bias_cur)
    bias_first = jnp.where(col < n, -jnp.inf, bias_cur)
    low = lax.broadcasted_iota(jnp.int32, (n, PAIR), 1) < HEAD_DIM
    low2 = lax.broadcasted_iota(jnp.int32, (2 * n, PAIR), 1) < HEAD_DIM
    scale = HEAD_DIM ** -0.5

    def rows(start, stride):
        return pl.ds(start, n) if stride == 1 else pl.ds(start, n, stride=stride)

    def prev_block(cur_ref, prev_ref, q0, stride, where):
        if isinstance(where, str):
            if where == "cur":
                return cur_ref[rows(q0 - n * stride, stride), :]
            return prev_ref[rows(tile + q0 - n * stride, stride), :]
        return jnp.where(where, cur_ref[rows(jnp.maximum(q0 - n, 0), 1), :], prev_ref[tile - n:tile, :])

    def run_units(units, stride, merge, last=False):
        loaded = []
        for q0, where, _ in units:
            qv = q_ref[rows(q0, stride), :] * scale
            kk = jnp.concatenate([prev_block(kc_ref, kp_ref, q0, stride, where),
                                  kc_ref[rows(q0, stride), :]], axis=0)
            vv = jnp.concatenate([prev_block(vc_ref, vp_ref, q0, stride, where),
                                  vc_ref[rows(q0, stride), :]], axis=0)
            qm = jnp.concatenate([jnp.where(low, qv, 0.0), jnp.where(low, 0.0, qv)], axis=0).astype(BF16)
            loaded.append((qm, kk.astype(BF16),
                           jnp.where(low2, vv, 1.0).astype(BF16), jnp.where(low2, 1.0, vv).astype(BF16)))
        scores = []
        for (qm, kk, _, _), (_, _, has_prev) in zip(loaded, units):
            b = bias if has_prev is None else jnp.where(has_prev, bias, bias_first)
            scores.append(_dot_nt(qm, kk) + b)
        m_new, alpha = [], []
        for s, (q0, _, _) in zip(scores, units):
            m_row = jnp.max(s, axis=-1, keepdims=True)
            if merge:
                m_old = jnp.concatenate([m0_ref[rows(q0, stride), :], m1_ref[rows(q0, stride), :]], axis=0)
                m = jnp.maximum(m_old, m_row)
                alpha.append(jnp.exp(m_old - m))
            else:
                m = jnp.broadcast_to(m_row, (2 * n, PAIR))
            m_new.append(m)
        probs = [jnp.exp(s - jnp.concatenate([m, m], axis=1)).astype(BF16) for s, m in zip(scores, m_new)]
        for i, (q0, _, _) in enumerate(units):
            _, _, va0, va1 = loaded[i]
            acc0 = _dot(probs[i][0:n], va0)
            acc1 = _dot(probs[i][n:], va1)
            if merge:
                acc0 = acc0 + alpha[i][0:n] * r0_ref[rows(q0, stride), :]
                acc1 = acc1 + alpha[i][n:] * r1_ref[rows(q0, stride), :]
            if last:
                num = jnp.where(low, acc0, acc1)
                den = pltpu.roll(jnp.where(low, acc1, acc0), HEAD_DIM, axis=1)
                o_ref[rows(q0, stride), :] = num / den
            else:
                r0_ref[rows(q0, stride), :] = acc0
                r1_ref[rows(q0, stride), :] = acc1
                m0_ref[rows(q0, stride), :] = m_new[i][0:n]
                m1_ref[rows(q0, stride), :] = m_new[i][n:]

    g_units = UNITS_PER_ITER
    not_first_tile = t > 0

    def branch_d16(it, carry):
        units = [(it * g_units + g, "prev", not_first_tile) for g in range(g_units)]
        run_units(units, 16, merge=False)
        return carry

    lax.fori_loop(0, 16 // g_units, branch_d16, 0)

    blocks_d4 = tile // (4 * n)
    streams_d4 = max(1, g_units // blocks_d4)

    def branch_d4(it, carry):
        units = [(i * (4 * n) + it * streams_d4 + st, "prev" if i == 0 else "cur",
                  not_first_tile if i == 0 else None)
                 for st in range(streams_d4) for i in range(blocks_d4)]
        run_units(units, 4, merge=True)
        return carry

    lax.fori_loop(0, 4 // streams_d4, branch_d4, 0)

    def branch_d1(it, carry):
        base = pl.multiple_of(it * (g_units * n), g_units * n)
        units = [(base, it > 0, jnp.logical_or(not_first_tile, it > 0))]
        units += [(base + g * n, "cur", None) for g in range(1, g_units)]
        run_units(units, 1, merge=True, last=True)
        return carry

    lax.fori_loop(0, tile // (g_units * n), branch_d1, 0)


def _attn_prompt(q, k, v):
    b, npair, s, w = q.shape
    assert [d for _, d in DILATED_BRANCHES] == [1, 4, 16] and s % ATT_TILE == 0
    tile = ATT_TILE
    cur = pl.BlockSpec((None, None, tile, w), lambda bi, p, t: (bi, p, t, 0))
    prev = pl.BlockSpec((None, None, tile, w), lambda bi, p, t: (bi, p, jnp.maximum(t - 1, 0), 0))
    return pl.pallas_call(
        _attn_prompt_body,
        grid=(b, npair, s // tile),
        in_specs=[cur, cur, prev, cur, prev],
        out_specs=cur,
        out_shape=jax.ShapeDtypeStruct((b, npair, s, w), F32),
        scratch_shapes=[pltpu.VMEM((tile, w), F32)] * 4,
        compiler_params=_params("arbitrary", "arbitrary", "arbitrary"),
        name="attn_prompt",
    )(q, k, k, v, v)


def _branch_count(dist):
    cnt = jnp.zeros(dist.shape, F32)
    for window, dil in DILATED_BRANCHES:
        hit = (dist >= 0) & (dist <= window) & ((dist & (dil - 1)) == 0)
        cnt = cnt + hit.astype(F32)
    return cnt


def _attn_sample_body(q_ref, kn_ref, vn_ref, kt_ref, vt_ref, att_ref, kto_ref, vto_ref, *, n_new):
    lb = kt_ref.shape[1]
    rows = q_ref.shape[1]
    k_new = kn_ref[...]
    v_new = vn_ref[...]

    tail_lane = lax.broadcasted_iota(jnp.int32, (64, LANES), 1) >= LANES - n_new
    for new, src_ref, dst_ref in ((k_new, kt_ref, kto_ref), (v_new, vt_ref, vto_ref)):
        last8 = pltpu.roll(new[0:8], 8 - n_new, axis=0)
        padded = jnp.concatenate([jnp.zeros((LANES - 8, ATT_WIDTH), F32), last8], axis=0)
        new_t = jnp.concatenate([padded[:, p * LANES:(p + 1) * LANES].T for p in range(N_PAIRS)], axis=0)
        for c in range(ATT_WIDTH // 64):
            shifted = pltpu.roll(src_ref[c * 64:(c + 1) * 64, :], lb - n_new, axis=1)
            dst_ref[c * 64:(c + 1) * 64, 0:lb - LANES] = shifted[:, 0:lb - LANES]
            dst_ref[c * 64:(c + 1) * 64, lb - LANES:lb] = jnp.where(
                tail_lane, new_t[c * 64:(c + 1) * 64], shifted[:, lb - LANES:lb])

    t_c = lax.broadcasted_iota(jnp.int32, (rows, lb), 0)
    i_c = lax.broadcasted_iota(jnp.int32, (rows, lb), 1)
    cnt_c = _branch_count(lb + t_c - i_c)
    t_n = lax.broadcasted_iota(jnp.int32, (rows, rows), 0)
    i_n = lax.broadcasted_iota(jnp.int32, (rows, rows), 1)
    cnt_n = jnp.where(i_n < n_new, _branch_count(t_n - i_n), 0.0)
    low = lax.broadcasted_iota(jnp.int32, (rows, PAIR), 1) < HEAD_DIM
    scale = HEAD_DIM ** -0.5
    for pr in range(N_PAIRS):
        sl = slice(pr * PAIR, (pr + 1) * PAIR)
        qp = q_ref[pr]
        ktp = kt_ref[sl, :].astype(BF16)
        vtp = vt_ref[sl, :].astype(BF16)
        knp = k_new[:, sl].astype(BF16)
        vnp = v_new[:, sl].astype(BF16)
        o_h = []
        for hh in range(2):
            qm = jnp.where(low if hh == 0 else jnp.logical_not(low), qp, 0.0).astype(BF16)
            s_c = jnp.where(cnt_c > 0, _dot(qm, ktp) * scale, -jnp.inf)
            s_n = jnp.where(cnt_n > 0, _dot_nt(qm, knp) * scale, -jnp.inf)
            m = jnp.maximum(jnp.max(s_c, axis=-1, keepdims=True), jnp.max(s_n, axis=-1, keepdims=True))
            p_c = cnt_c * jnp.exp(s_c - m)
            p_n = cnt_n * jnp.exp(s_n - m)
            den = jnp.sum(p_c, axis=-1, keepdims=True) + jnp.sum(p_n, axis=-1, keepdims=True)
            pv = _dot_nt(p_c.astype(BF16), vtp) + _dot(p_n.astype(BF16), vnp)
            o_h.append(pv / den)
        att_ref[pr] = jnp.where(low, o_h[0], o_h[1])


def _attn_sample(q, k_new, v_new, kt_cache, vt_cache, rows, n_new):
    b, w, lb = kt_cache.shape
    assert n_new <= 8 and lb % LANES == 0
    slab = pl.BlockSpec((None, N_PAIRS, rows, PAIR), lambda bi: (0, 0, bi, 0))
    new = pl.BlockSpec((None, rows, w), lambda bi: (0, bi, 0))
    cache = pl.BlockSpec((None, w, lb), lambda bi: (bi, 0, 0))
    return pl.pallas_call(
        functools.partial(_attn_sample_body, n_new=n_new),
        grid=(b,),
        in_specs=[slab, new, new, cache, cache],
        out_specs=(slab, cache, cache),
        out_shape=(jax.ShapeDtypeStruct(q.shape, F32),
                   jax.ShapeDtypeStruct((b, w, lb), F32),
                   jax.ShapeDtypeStruct((b, w, lb), F32)),
        compiler_params=_params("arbitrary"),
        name="attn_sample",
    )(q, k_new, v_new, kt_cache, vt_cache)


def _ssd_body(xbc_ref, z_ref, dt_ref, cprev_ref, h0_ref, cw_ref, cb_ref, dtb_ref, alog_ref, dskip_ref, g_ref,
              y_ref, hout_ref, h_ref, xprev_ref, *, q, chunks, n_valid):
    @pl.when(pl.program_id(1) == 0)
    def _():
        h_ref[...] = h0_ref[0]
        xprev_ref[...] = cprev_ref[0]

    def one_chunk(ck, carry):
        _ssd_chunk(pl.program_id(1) * chunks + ck, pl.multiple_of(ck * q, q),
                   xbc_ref, z_ref, dt_ref, cw_ref, cb_ref, dtb_ref, alog_ref, dskip_ref, g_ref, y_ref,
                   h_ref, xprev_ref, q=q, n_valid=n_valid)
        return carry

    lax.fori_loop(0, chunks, one_chunk, 0)
    hout_ref[0] = h_ref[...]


def _silu(x):
    half = 0.5 * x
    return half + half * jnp.tanh(half)


def _ssd_chunk(chunk, r0, xbc_ref, z_ref, dt_ref, cw_ref, cb_ref, dtb_ref, alog_ref, dskip_ref, g_ref, y_ref,
               h_ref, xprev_ref, *, q, n_valid):
    def chunk_rows(ref):
        return ref[0, pl.ds(r0, q), :]

    def to_columns(t8):
        return jnp.concatenate([t8, jnp.zeros((LANES - 8, q), F32)], axis=0).T

    cur = chunk_rows(xbc_ref)
    prev8 = xprev_ref[...]
    i8 = lax.broadcasted_iota(jnp.int32, (8, CONV_CH), 0)
    acc = cb_ref[...] + cur * cw_ref[CONV_WIDTH - 1:CONV_WIDTH, :]
    for sh in range(1, CONV_WIDTH):
        rolled = pltpu.roll(cur, sh, axis=0)
        head = jnp.where(i8 < sh, pltpu.roll(prev8, sh, axis=0), rolled[0:8])
        shifted = jnp.concatenate([head, rolled[8:]], axis=0)
        acc = acc + shifted * cw_ref[CONV_WIDTH - 1 - sh:CONV_WIDTH - sh, :]
    xprev_ref[...] = cur[q - 8:q]
    conv = _silu(acc)
    xs = conv[:, 0:SSM_WIDTH]

    tok = lax.broadcasted_iota(jnp.int32, (N_SSM_HEADS, q), 1)
    dt_in = chunk_rows(dt_ref).T[0:N_SSM_HEADS, :] + dtb_ref[:, 0:q]
    dtv_t = jnp.maximum(dt_in, 0.0) + jnp.log1p(jnp.exp(-jnp.abs(dt_in)))
    dtv_t = jnp.where(chunk * q + tok < n_valid, dtv_t, 0.0)
    a_t = dtv_t * (-jnp.exp(alog_ref[:, 0:q]))
    row_l = lax.broadcasted_iota(jnp.int32, (q, q), 0)
    lane_l = lax.broadcasted_iota(jnp.int32, (q, q), 1)
    tri = row_l >= lane_l
    a_cs_t = jnp.dot(a_t, (row_l <= lane_l).astype(F32), preferred_element_type=F32,
                     precision=lax.Precision.HIGHEST)
    a_cs = to_columns(a_cs_t)
    dtv = to_columns(dtv_t)
    low = lax.broadcasted_iota(jnp.int32, (q, PAIR), 1) < HEAD_DIM

    def per_pair(t, pr):
        return jnp.where(low, t[:, 2 * pr:2 * pr + 1], t[:, 2 * pr + 1:2 * pr + 2])

    n_pairs = N_SSM_HEADS // 2
    pairs_per_group = n_pairs // SSM_GROUPS
    decay_in = jnp.exp(a_cs)
    decay_out = jnp.exp(a_cs[q - 1:q, :] - a_cs)
    total = jnp.exp(a_cs_t[:, q - 1:q])
    bmb, cmb, cb = [], [], []
    for grp in range(SSM_GROUPS):
        b0 = SSM_WIDTH + grp * SSM_STATE
        c0 = SSM_WIDTH + (SSM_GROUPS + grp) * SSM_STATE
        bmb.append(conv[:, b0:b0 + SSM_STATE].astype(BF16))
        cmb.append(conv[:, c0:c0 + SSM_STATE].astype(BF16))
        cb.append(_dot_nt(cmb[grp], bmb[grp]))
    xs_p = [xs[:, pr * PAIR:(pr + 1) * PAIR] for pr in range(n_pairs)]
    xdt = [xs_p[pr] * per_pair(dtv, pr) for pr in range(n_pairs)]
    xdt_b = [t.astype(BF16) for t in xdt]
    scores = [(cb[hd // (N_SSM_HEADS // SSM_GROUPS)]
               * jnp.exp(jnp.where(tri, a_cs[:, hd:hd + 1] - a_cs_t[hd:hd + 1, :], -jnp.inf))).astype(BF16)
              for hd in range(N_SSM_HEADS)]
    y_diag = [jnp.where(low, _dot(scores[2 * pr], xdt_b[pr]), _dot(scores[2 * pr + 1], xdt_b[pr]))
              for pr in range(n_pairs)]
    states = [_dot_tn((xdt[pr] * per_pair(decay_out, pr)).astype(BF16), bmb[pr // pairs_per_group])
              for pr in range(n_pairs)]
    h_old = [h_ref[pr * PAIR:(pr + 1) * PAIR, :] for pr in range(n_pairs)]
    y_off = [_dot_nt(cmb[pr // pairs_per_group], h_old[pr].astype(BF16)) * per_pair(decay_in, pr)
             for pr in range(n_pairs)]
    ys = []
    for pr in range(n_pairs):
        chunk_decay = jnp.concatenate([jnp.broadcast_to(total[2 * pr:2 * pr + 1], (HEAD_DIM, SSM_STATE)),
                                       jnp.broadcast_to(total[2 * pr + 1:2 * pr + 2], (HEAD_DIM, SSM_STATE))],
                                      axis=0)
        h_ref[pr * PAIR:(pr + 1) * PAIR, :] = h_old[pr] * chunk_decay + states[pr]
        ys.append(y_diag[pr] + y_off[pr] + dskip_ref[:, pr * PAIR:(pr + 1) * PAIR] * xs_p[pr])
    y = jnp.concatenate(ys, axis=1)

    u = y * _silu(chunk_rows(z_ref))
    gw = SSM_WIDTH // SSM_GROUPS
    normed = []
    for grp in range(SSM_GROUPS):
        ug = u[:, grp * gw:(grp + 1) * gw]
        normed.append(ug * lax.rsqrt(jnp.mean(ug * ug, axis=-1, keepdims=True) + EPS))
    out = jnp.concatenate(normed, axis=1) * g_ref[...]
    y_ref[0, pl.ds(r0, q), :] = out


def _ssd(xbc, z, dt, conv_prev8, h0, conv_w8, conv_b, dt_bias, a_log, d_skip, g_ssm, n_valid):
    b, l, _ = xbc.shape
    q = min(SSD_CHUNK, l)
    chunks = 4 if l % (4 * q) == 0 else 1
    rows_in = q * chunks
    nc = l // rows_in
    row = lambda bi, c: (bi, c, 0)
    per_b = lambda bi, c: (bi, 0, 0)
    return pl.pallas_call(
        functools.partial(_ssd_body, q=q, chunks=chunks, n_valid=n_valid),
        grid=(b, nc),
        in_specs=[
            pl.BlockSpec((1, rows_in, CONV_CH), row),
            pl.BlockSpec((1, rows_in, SSM_WIDTH), row),
            pl.BlockSpec((1, rows_in, LANES), row),
            pl.BlockSpec((1, 8, CONV_CH), per_b),
            pl.BlockSpec((1, SSM_WIDTH, SSM_STATE), per_b),
            _const_spec((8, CONV_CH)),
            _const_spec((1, CONV_CH)),
            _const_spec((N_SSM_HEADS, SSD_CHUNK)),
            _const_spec((N_SSM_HEADS, SSD_CHUNK)),
            _const_spec((1, SSM_WIDTH)),
            _const_spec((1, SSM_WIDTH)),
        ],
        out_specs=(pl.BlockSpec((1, rows_in, SSM_WIDTH), row),
                   pl.BlockSpec((1, SSM_WIDTH, SSM_STATE), per_b)),
        out_shape=(jax.ShapeDtypeStruct((b, l, SSM_WIDTH), F32),
                   jax.ShapeDtypeStruct((b, SSM_WIDTH, SSM_STATE), F32)),
        scratch_shapes=[pltpu.VMEM((SSM_WIDTH, SSM_STATE), F32), pltpu.VMEM((8, CONV_CH), F32)],
        compiler_params=_params("arbitrary", "arbitrary"),
        name="ssd",
    )(xbc, z, dt, conv_prev8, h0, conv_w8, conv_b, dt_bias, a_log, d_skip, g_ssm)


def _post1_body(x_ref, att_ref, y_ref, wout_ref, g_ref, wxq_ref, x1_ref, qx_ref):
    mixed = jnp.concatenate([att_ref[0, pr] for pr in range(N_PAIRS)] + [y_ref[0]], axis=1).astype(BF16)
    x1 = x_ref[0] + _dot(mixed, wout_ref[...])
    x1_ref[0] = x1
    qx_ref[0] = _dot(_rmsnorm(x1, g_ref[...]).astype(BF16), wxq_ref[...]).astype(BF16)


def _post1(x, att, y, w_out, g_xatt, w_xq, tm):
    b, l, _ = x.shape
    row = lambda bi, j: (bi, j, 0)
    full = pl.BlockSpec((1, tm, D_MODEL), row)
    return pl.pallas_call(
        _post1_body,
        grid=(b, l // tm),
        in_specs=[full,
                  pl.BlockSpec((1, N_PAIRS, tm, PAIR), lambda bi, j: (bi, 0, j, 0)),
                  pl.BlockSpec((1, tm, SSM_WIDTH), row),
                  _const_spec(w_out.shape), _const_spec((1, D_MODEL)), _const_spec(w_xq.shape)],
        out_specs=(full, full),
        out_shape=(jax.ShapeDtypeStruct((b, l, D_MODEL), F32), jax.ShapeDtypeStruct((b, l, D_MODEL), BF16)),
        compiler_params=_params("arbitrary", "arbitrary"),
        name="post1",
    )(x, att, y, w_out, g_xatt, w_xq)


def _mem_kv_body(m_ref, g_ref, wk_ref, wv_ref, k_ref, v_ref, kb_ref, vb_ref):
    h = _rmsnorm(m_ref[...], g_ref[...]).astype(BF16)
    k = _dot(h, wk_ref[...])
    v = _dot(h, wv_ref[...])
    k_ref[...] = k
    v_ref[...] = v
    kb_ref[...] = k.astype(BF16)
    vb_ref[...] = v.astype(BF16)


def _mem_kv(mem, g_mem, w_mk, w_mv, tm):
    n, _ = mem.shape
    row = pl.BlockSpec((tm, D_MODEL), lambda i: (i, 0))
    return pl.pallas_call(
        _mem_kv_body,
        grid=(n // tm,),
        in_specs=[row, _const_spec((1, D_MODEL)), _const_spec(w_mk.shape), _const_spec(w_mv.shape)],
        out_specs=(row, row, row, row),
        out_shape=(jax.ShapeDtypeStruct((n, D_MODEL), F32), jax.ShapeDtypeStruct((n, D_MODEL), F32),
                   jax.ShapeDtypeStruct((n, D_MODEL), BF16), jax.ShapeDtypeStruct((n, D_MODEL), BF16)),
        compiler_params=_params("arbitrary"),
        name="mem_kv",
    )(mem, g_mem, w_mk, w_mv)


def _xattn_body(q_ref, mk_ref, mv_ref, o_ref, *, tiled):
    scale = XATT_HEAD_DIM ** -0.5
    chunks = XATT_HEAD_DIM // LANES

    def head(ref, gi, hd):
        if not tiled:
            return ref[gi, :, hd * XATT_HEAD_DIM:(hd + 1) * XATT_HEAD_DIM]
        return jnp.concatenate(
            [ref[gi, pl.ds(c * N_XATT_HEADS + hd, N_MEM, stride=chunks * N_XATT_HEADS), :] for c in range(chunks)],
            axis=1).astype(BF16)

    for gi in range(q_ref.shape[0]):
        for hd in range(N_XATT_HEADS):
            sl = slice(hd * XATT_HEAD_DIM, (hd + 1) * XATT_HEAD_DIM)
            s = _dot_nt(q_ref[gi, :, sl], head(mk_ref, gi, hd)) * scale
            p = jnp.exp(s - jnp.max(s, axis=-1, keepdims=True))
            den = jnp.sum(p, axis=-1, keepdims=True)
            o = _dot(p.astype(BF16), head(mv_ref, gi, hd)) / den
            o_ref[gi, :, sl] = o.astype(BF16)


def _xattn(qx, mem_k, mem_v, gb, tm):
    b, l, _ = qx.shape
    tiled = mem_k.shape[-1] == LANES
    qspec = pl.BlockSpec((gb, tm, D_MODEL), lambda bi, j: (bi, j, 0))
    mspec = pl.BlockSpec((gb,) + mem_k.shape[1:], lambda bi, j: (bi, 0, 0))
    return pl.pallas_call(
        functools.partial(_xattn_body, tiled=tiled),
        grid=(b // gb, l // tm),
        in_specs=[qspec, mspec, mspec],
        out_specs=qspec,
        out_shape=jax.ShapeDtypeStruct((b, l, D_MODEL), BF16),
        compiler_params=_params("arbitrary", "arbitrary"),
        name="xattn",
    )(qx, mem_k, mem_v)


def _post2_body(x1_ref, o_ref, wxo_ref, gmlp_ref, wup_ref, wdown_ref, gfin_ref, y_ref):
    x2 = x1_ref[...] + _dot(o_ref[...], wxo_ref[...])
    hm = _rmsnorm(x2, gmlp_ref[...]).astype(BF16)
    acc = x2
    for c in range(D_FF // D_MODEL):
        sl = slice(c * D_MODEL, (c + 1) * D_MODEL)
        u = jnp.maximum(_dot(hm, wup_ref[:, sl]), 0.0)
        acc = acc + _dot((u * u).astype(BF16), wdown_ref[sl, :])
    y_ref[...] = _rmsnorm(acc, gfin_ref[...])


def _post2(x1, o, w_xo, g_mlp, w_up, w_down, g_final, tm):
    n, _ = x1.shape
    row = pl.BlockSpec((tm, D_MODEL), lambda i: (i, 0))
    return pl.pallas_call(
        _post2_body,
        grid=(n // tm,),
        in_specs=[row, row, _const_spec(w_xo.shape), _const_spec((1, D_MODEL)), _const_spec(w_up.shape),
                  _const_spec(w_down.shape), _const_spec((1, D_MODEL))],
        out_specs=row,
        out_shape=jax.ShapeDtypeStruct((n, D_MODEL), F32),
        compiler_params=_params("arbitrary"),
        name="post2",
    )(x1, o, w_xo, g_mlp, w_up, w_down, g_final)


def _rope_tables(pos):
    half = HEAD_DIM // 2
    inv = ROPE_THETA ** (-np.arange(half, dtype=np.float64) * 2.0 / HEAD_DIM)
    ang = np.asarray(pos, np.float64)[:, None] * inv[None, :]
    c, s = np.cos(ang), np.sin(ang)
    return (jnp.asarray(np.concatenate([c, c, c, c], axis=1), F32),
            jnp.asarray(np.concatenate([-s, s, -s, s], axis=1), F32))


def _pad_lanes(t):
    return jnp.pad(t, ((0, 0), (0, LANES - t.shape[1])))


def kernel(x_prompt, x_sample, cache_win_k, cache_win_v, state_conv, state_ssm, cache_mem_k, cache_mem_v,
           mem_prompt, g_mix, w_in, conv_w, conv_b, dt_bias, a_log, d_skip, g_ssm, w_out, g_xatt, g_mem,
           w_xq, w_mk, w_mv, w_xo, g_mlp, w_up, w_down, g_final):
    depth = w_in.shape[0]
    assert depth == 1, "kernel is written for the single-layer trunk of this problem"
    bp, s_len, _ = x_prompt.shape
    bs, t_new, _ = x_sample.shape
    lb = cache_win_k.shape[2]
    lw = min(MAX_WINDOW, s_len)
    n_proj = ATT_WIDTH * 3 + SSM_WIDTH + CONV_CH

    li = 0
    row = lambda t: t[li].reshape(1, -1)
    w_main = w_in[li, :, :n_proj].astype(BF16)
    w_dt = _pad_lanes(w_in[li, :, n_proj:]).astype(BF16)
    conv_w8 = jnp.pad(conv_w[li], ((0, 8 - CONV_WIDTH), (0, 0)))
    per_head_rows = lambda t: jnp.broadcast_to(t[li][:, None], (N_SSM_HEADS, SSD_CHUNK))
    dt_b = per_head_rows(dt_bias)
    a_lg = per_head_rows(a_log)
    d_sk = jnp.repeat(d_skip[li], HEAD_DIM).reshape(1, -1)
    w_o, w_q, w_k, w_v, w_x = (t[li].astype(BF16) for t in (w_out, w_xq, w_mk, w_mv, w_xo))
    w_u, w_d = w_up[li].astype(BF16), w_down[li].astype(BF16)
    g_fin = g_final.reshape(1, -1)

    cos_p, sin_p = _rope_tables(np.arange(s_len))
    q, k, v, k32, v32, z, xbc, dt = _in_proj(x_prompt, row(g_mix), w_main, w_dt, cos_p, sin_p, lw, ROW_TILE)
    att = _attn_prompt(q, k, v)
    y_ssm, ssm_p = _ssd(xbc, z, dt, jnp.zeros((bp, 8, CONV_CH), F32),
                        jnp.zeros((bp, SSM_WIDTH, SSM_STATE), F32), conv_w8, row(conv_b), dt_b, a_lg, d_sk,
                        row(g_ssm), s_len)
    mk, mv, mk_b, mv_b = _mem_kv(mem_prompt.reshape(bp * N_MEM, D_MODEL), row(g_mem), w_k, w_v, ROW_TILE)
    x1, qx = _post1(x_prompt, att, y_ssm, w_o, row(g_xatt), w_q, ROW_TILE)
    o = _xattn(qx, mk_b.reshape(bp, N_MEM, D_MODEL), mv_b.reshape(bp, N_MEM, D_MODEL), 1, ROW_TILE)
    y_prompt = _post2(x1.reshape(bp * s_len, D_MODEL), o.reshape(bp * s_len, D_MODEL), w_x, row(g_mlp),
                      w_u, w_d, g_fin, ROW_TILE)
    y_prompt = y_prompt.reshape(bp, s_len, D_MODEL)
    win_k_p = k32.reshape(1, bp, lw, N_ATT_HEADS, HEAD_DIM)
    win_v_p = v32.reshape(1, bp, lw, N_ATT_HEADS, HEAD_DIM)
    conv_p = xbc[:, s_len - (CONV_WIDTH - 1):][None]
    ssm_p = ssm_p.reshape(1, bp, N_SSM_HEADS, HEAD_DIM, SSM_STATE)
    mk_p = mk.reshape(1, bp, N_MEM, N_XATT_HEADS, XATT_HEAD_DIM)
    mv_p = mv.reshape(1, bp, N_MEM, N_XATT_HEADS, XATT_HEAD_DIM)

    r = SAMPLE_ROWS
    n_s = bs * r
    xs_pad = jnp.pad(x_sample, ((0, 0), (0, r - t_new), (0, 0))).reshape(1, n_s, D_MODEL)
    pos_s = PAST_LEN + (np.arange(n_s) % r)
    cos_s, sin_s = _rope_tables(pos_s)
    q, _, _, k32, v32, z, xbc, dt = _in_proj(xs_pad, row(g_mix), w_main, w_dt, cos_s, sin_s, n_s, ROW_TILE)
    to_minor = lambda c: jnp.transpose(c[li].reshape(bs, lb, ATT_WIDTH), (0, 2, 1))
    att, kt_s, vt_s = _attn_sample(q, k32, v32, to_minor(cache_win_k), to_minor(cache_win_v), r, t_new)
    per_seq = lambda t: t.reshape(bs, r, t.shape[-1])
    conv_prev8 = jnp.pad(state_conv[li], ((0, 0), (8 - (CONV_WIDTH - 1), 0), (0, 0)))
    xbc_s = per_seq(xbc)
    y_ssm, ssm_s = _ssd(xbc_s, per_seq(z), per_seq(dt), conv_prev8,
                        state_ssm[li].reshape(bs, SSM_WIDTH, SSM_STATE), conv_w8, row(conv_b), dt_b, a_lg, d_sk,
                        row(g_ssm), t_new)
    x1, qx = _post1(xs_pad, att, y_ssm.reshape(1, n_s, SSM_WIDTH), w_o, row(g_xatt), w_q, ROW_TILE)
    mem_rows = lambda c: c[li].reshape(bs, N_MEM, N_XATT_HEADS, XATT_HEAD_DIM // LANES, LANES).transpose(
        0, 1, 3, 2, 4).reshape(bs, N_MEM * N_XATT_HEADS * (XATT_HEAD_DIM // LANES), LANES)
    o = _xattn(qx.reshape(bs, r, D_MODEL), mem_rows(cache_mem_k), mem_rows(cache_mem_v), 4, r)
    y_s = _post2(x1.reshape(n_s, D_MODEL), o.reshape(n_s, D_MODEL), w_x, row(g_mlp), w_u, w_d, g_fin, ROW_TILE)
    y_sample = y_s.reshape(bs, r, D_MODEL)[:, :t_new]
    from_minor = lambda c: jnp.transpose(c, (0, 2, 1)).reshape(1, bs, lb, N_ATT_HEADS, HEAD_DIM)
    win_k_s, win_v_s = from_minor(kt_s), from_minor(vt_s)
    conv_s = jnp.concatenate([state_conv[li], xbc_s[:, :t_new]], axis=1)[:, -(CONV_WIDTH - 1):][None]
    ssm_s = ssm_s.reshape(1, bs, N_SSM_HEADS, HEAD_DIM, SSM_STATE)

    return (y_prompt, y_sample, win_k_p, win_v_p, conv_p, ssm_p, mk_p, mv_p,
            win_k_s, win_v_s, conv_s, ssm_s)
```

```python
import functools

import jax
import jax.numpy as jnp
import numpy as np
from jax import lax
from jax.experimental import pallas as pl
from jax.experimental.pallas import tpu as pltpu

F32 = jnp.float32
BF16 = jnp.bfloat16

D_MODEL = 1024
PAST_LEN = 8192
HEAD_DIM = 64
ATT_WIDTH = 512
N_ATT_HEADS = 8
DILATED_BRANCHES = ((128, 1), (512, 4), (2048, 16))
MAX_WINDOW = 2048
ROPE_THETA = 10000.0
SSM_WIDTH = 512
N_SSM_HEADS = 8
SSM_GROUPS = 2
SSM_STATE = 128
CONV_WIDTH = 4
SSD_CHUNK = 128
CONV_CH = 1024
N_MEM = 256
N_XATT_HEADS = 4
XATT_HEAD_DIM = 256
D_FF = 4096
EPS = 1e-6

LANES = 128
DT_LANE0 = LANES - N_SSM_HEADS
PAIR = 2 * HEAD_DIM
N_PAIRS = ATT_WIDTH // PAIR
BRANCH_BLOCK = 128
ATT_TILE = 2048
UNITS_PER_ITER = 4
ROW_TILE = 512
SAMPLE_ROWS = 16
VMEM_LIMIT = 56 * 1024 * 1024


def _params(*sem):
    return pltpu.CompilerParams(dimension_semantics=sem, vmem_limit_bytes=VMEM_LIMIT)


def _const_spec(shape):
    return pl.BlockSpec(shape, lambda *_: (0,) * len(shape), pipeline_mode=pl.Buffered(1))


def _rmsnorm(x, g):
    return x * lax.rsqrt(jnp.mean(x * x, axis=-1, keepdims=True) + EPS) * g


def _dot(a, b):
    return jnp.dot(a, b, preferred_element_type=F32)


def _dot_nt(a, b):
    return lax.dot_general(a, b, (((1,), (1,)), ((), ())), preferred_element_type=F32)


def _dot_tn(a, b):
    return lax.dot_general(a, b, (((0,), (0,)), ((), ())), preferred_element_type=F32)


def _in_proj_body(x_ref, g_ref, wt_ref, cos_ref, sin_ref,
                  q_ref, k_ref, v_ref, k32_ref, v32_ref, z_ref, xbc_ref, dt_ref):
    h = _rmsnorm(x_ref[0], g_ref[...]).astype(BF16)
    tm = h.shape[0]
    cos = jnp.concatenate([cos_ref[...]] * N_PAIRS, axis=1)
    sin = jnp.concatenate([sin_ref[...]] * N_PAIRS, axis=1)
    lane = lax.broadcasted_iota(jnp.int32, (tm, ATT_WIDTH), 1)
    first_half = (lane & (HEAD_DIM // 2)) == 0

    def rope(t):
        partner = jnp.where(first_half,
                            pltpu.roll(t, ATT_WIDTH - HEAD_DIM // 2, axis=1),
                            pltpu.roll(t, HEAD_DIM // 2, axis=1))
        return t * cos + partner * sin

    def proj(lo, hi):
        return _dot_nt(h, wt_ref[lo:hi, :].astype(BF16))

    q = rope(proj(0, 512))
    k = rope(proj(512, 1024))
    v = proj(1024, 1536)
    for pr in range(N_PAIRS):
        sl = slice(pr * PAIR, (pr + 1) * PAIR)
        q_ref[0, pr] = q[:, sl]
        k_ref[0, pr] = k[:, sl]
        v_ref[0, pr] = v[:, sl]
    k32_ref[0] = k
    v32_ref[0] = v
    z_ref[0] = proj(1536, 2048)
    xbc_ref[0] = proj(2048, 3072)
    n_in = wt_ref.shape[0]
    dt_ref[0] = proj(n_in - LANES, n_in)


def _in_proj(x, g, w_t, cos, sin, window, tm):
    b, l, _ = x.shape
    nj = l // tm
    j0 = (l - window) // tm
    row = lambda bi, j: (bi, j, 0)
    slab = lambda bi, j: (bi, 0, j, 0)
    win = lambda bi, j: (bi, jnp.maximum(j - j0, 0), 0)
    slab_shape = jax.ShapeDtypeStruct((b, N_PAIRS, l, PAIR), F32)
    slab_spec = pl.BlockSpec((1, N_PAIRS, tm, PAIR), slab)
    outs = (
        slab_shape, slab_shape, slab_shape,
        jax.ShapeDtypeStruct((b, window, ATT_WIDTH), F32),
        jax.ShapeDtypeStruct((b, window, ATT_WIDTH), F32),
        jax.ShapeDtypeStruct((b, l, SSM_WIDTH), F32),
        jax.ShapeDtypeStruct((b, l, CONV_CH), F32),
        jax.ShapeDtypeStruct((b, l, LANES), F32),
    )
    return pl.pallas_call(
        _in_proj_body,
        grid=(b, nj),
        in_specs=[
            pl.BlockSpec((1, tm, D_MODEL), row),
            _const_spec((1, D_MODEL)),
            _const_spec(w_t.shape),
            pl.BlockSpec((tm, LANES), lambda bi, j: (j, 0)),
            pl.BlockSpec((tm, LANES), lambda bi, j: (j, 0)),
        ],
        out_specs=(
            slab_spec, slab_spec, slab_spec,
            pl.BlockSpec((1, tm, ATT_WIDTH), win),
            pl.BlockSpec((1, tm, ATT_WIDTH), win),
            pl.BlockSpec((1, tm, SSM_WIDTH), row),
            pl.BlockSpec((1, tm, CONV_CH), row),
            pl.BlockSpec((1, tm, LANES), row),
        ),
        out_shape=outs,
        compiler_params=_params("arbitrary", "arbitrary"),
        name="in_proj",
    )(x, g, w_t, cos, sin)


def _attn_prompt_body(q_ref, kc_ref, kp_ref, vc_ref, vp_ref, o_ref, r0_ref, r1_ref, m0_ref, m1_ref):
    t = pl.program_id(2)
    n = BRANCH_BLOCK
    tile = ATT_TILE

    row = lax.broadcasted_iota(jnp.int32, (2 * n, 2 * n), 0) & (n - 1)
    col = lax.broadcasted_iota(jnp.int32, (2 * n, 2 * n), 1)
    bias_cur = jnp.where(col - n <= row, 0.0, -jnp.inf)
    bias = jnp.where(col < n, jnp.where(col >= row, 0.0, -jnp.inf), bias_cur)
    bias_first = jnp.where(col < n, -jnp.inf, bias_cur)
    low = lax.broadcasted_iota(jnp.int32, (n, PAIR), 1) < HEAD_DIM
    low2 = lax.broadcasted_iota(jnp.int32, (2 * n, PAIR), 1) < HEAD_DIM
    scale = HEAD_DIM ** -0.5

    def rows(start, stride):
        return pl.ds(start, n) if stride == 1 else pl.ds(start, n, stride=stride)

    def prev_block(cur_ref, prev_ref, q0, stride, where):
        if isinstance(where, str):
            if where == "cur":
                return cur_ref[rows(q0 - n * stride, stride), :]
            return prev_ref[rows(tile + q0 - n * stride, stride), :]
        return jnp.where(where, cur_ref[rows(jnp.maximum(q0 - n, 0), 1), :], prev_ref[tile - n:tile, :])

    def run_units(units, stride, merge, last=False):
        loaded = []
        for q0, where, _ in units:
            qv = q_ref[rows(q0, stride), :] * scale
            kk = jnp.concatenate([prev_block(kc_ref, kp_ref, q0, stride, where),
                                  kc_ref[rows(q0, stride), :]], axis=0)
            vv = jnp.concatenate([prev_block(vc_ref, vp_ref, q0, stride, where),
                                  vc_ref[rows(q0, stride), :]], axis=0)
            qm = jnp.concatenate([jnp.where(low, qv, 0.0), jnp.where(low, 0.0, qv)], axis=0).astype(BF16)
            loaded.append((qm, kk.astype(BF16),
                           jnp.where(low2, vv, 1.0).astype(BF16), jnp.where(low2, 1.0, vv).astype(BF16)))
        scores = []
        for (qm, kk, _, _), (_, _, has_prev) in zip(loaded, units):
            b = bias if has_prev is None else jnp.where(has_prev, bias, bias_first)
            scores.append(_dot_nt(qm, kk) + b)
        m_new, alpha = [], []
        for s, (q0, _, _) in zip(scores, units):
            m_row = jnp.max(s, axis=-1, keepdims=True)
            if merge:
                m_old = jnp.concatenate([m0_ref[rows(q0, stride), :], m1_ref[rows(q0, stride), :]], axis=0)
                m = jnp.maximum(m_old, m_row)
                alpha.append(jnp.exp(m_old - m))
            else:
                m = jnp.broadcast_to(m_row, (2 * n, PAIR))
            m_new.append(m)
        probs = [jnp.exp(s - jnp.concatenate([m, m], axis=1)).astype(BF16) for s, m in zip(scores, m_new)]
        for i, (q0, _, _) in enumerate(units):
            _, _, va0, va1 = loaded[i]
            acc0 = _dot(probs[i][0:n], va0)
            acc1 = _dot(probs[i][n:], va1)
            if merge:
                acc0 = acc0 + alpha[i][0:n] * r0_ref[rows(q0, stride), :]
                acc1 = acc1 + alpha[i][n:] * r1_ref[rows(q0, stride), :]
            if last:
                num = jnp.where(low, acc0, acc1)
                den = pltpu.roll(jnp.where(low, acc1, acc0), HEAD_DIM, axis=1)
                o_ref[rows(q0, stride), :] = num / den
            else:
                r0_ref[rows(q0, stride), :] = acc0
                r1_ref[rows(q0, stride), :] = acc1
                m0_ref[rows(q0, stride), :] = m_new[i][0:n]
                m1_ref[rows(q0, stride), :] = m_new[i][n:]

    g_units = UNITS_PER_ITER
    not_first_tile = t > 0

    def branch_d16(it, carry):
        units = [(it * g_units + g, "prev", not_first_tile) for g in range(g_units)]
        run_units(units, 16, merge=False)
        return carry

    lax.fori_loop(0, 16 // g_units, branch_d16, 0)

    blocks_d4 = tile // (4 * n)
    streams_d4 = max(1, g_units // blocks_d4)

    def branch_d4(it, carry):
        units = [(i * (4 * n) + it * streams_d4 + st, "prev" if i == 0 else "cur",
                  not_first_tile if i == 0 else None)
                 for st in range(streams_d4) for i in range(blocks_d4)]
        run_units(units, 4, merge=True)
        return carry

    lax.fori_loop(0, 4 // streams_d4, branch_d4, 0)

    def branch_d1(it, carry):
        base = pl.multiple_of(it * (g_units * n), g_units * n)
        units = [(base, it > 0, jnp.logical_or(not_first_tile, it > 0))]
        units += [(base + g * n, "cur", None) for g in range(1, g_units)]
        run_units(units, 1, merge=True, last=True)
        return carry

    lax.fori_loop(0, tile // (g_units * n), branch_d1, 0)


def _attn_prompt(q, k, v):
    b, npair, s, w = q.shape
    assert [d for _, d in DILATED_BRANCHES] == [1, 4, 16] and s % ATT_TILE == 0
    tile = ATT_TILE
    cur = pl.BlockSpec((None, None, tile, w), lambda bi, p, t: (bi, p, t, 0))
    prev = pl.BlockSpec((None, None, tile, w), lambda bi, p, t: (bi, p, jnp.maximum(t - 1, 0), 0))
    return pl.pallas_call(
        _attn_prompt_body,
        grid=(b, npair, s // tile),
        in_specs=[cur, cur, prev, cur, prev],
        out_specs=cur,
        out_shape=jax.ShapeDtypeStruct((b, npair, s, w), F32),
        scratch_shapes=[pltpu.VMEM((tile, w), F32)] * 4,
        compiler_params=_params("arbitrary", "arbitrary", "arbitrary"),
        name="attn_prompt",
    )(q, k, k, v, v)


def _branch_count(dist):
    cnt = jnp.zeros(dist.shape, F32)
    for window, dil in DILATED_BRANCHES:
        hit = (dist >= 0) & (dist <= window) & ((dist & (dil - 1)) == 0)
        cnt = cnt + hit.astype(F32)
    return cnt


def _attn_sample_body(q_ref, kn_ref, vn_ref, kt_ref, vt_ref, att_ref, kto_ref, vto_ref, *, n_new):
    lb = kt_ref.shape[1]
    rows = q_ref.shape[1]
    k_new = kn_ref[...]
    v_new = vn_ref[...]

    tail_lane = lax.broadcasted_iota(jnp.int32, (64, LANES), 1) >= LANES - n_new
    for new, src_ref, dst_ref in ((k_new, kt_ref, kto_ref), (v_new, vt_ref, vto_ref)):
        last8 = pltpu.roll(new[0:8], 8 - n_new, axis=0)
        padded = jnp.concatenate([jnp.zeros((LANES - 8, ATT_WIDTH), F32), last8], axis=0)
        new_t = jnp.concatenate([padded[:, p * LANES:(p + 1) * LANES].T for p in range(N_PAIRS)], axis=0)
        for c in range(ATT_WIDTH // 64):
            shifted = pltpu.roll(src_ref[c * 64:(c + 1) * 64, :], lb - n_new, axis=1)
            dst_ref[c * 64:(c + 1) * 64, 0:lb - LANES] = shifted[:, 0:lb - LANES]
            dst_ref[c * 64:(c + 1) * 64, lb - LANES:lb] = jnp.where(
                tail_lane, new_t[c * 64:(c + 1) * 64], shifted[:, lb - LANES:lb])

    t_c = lax.broadcasted_iota(jnp.int32, (rows, lb), 0)
    i_c = lax.broadcasted_iota(jnp.int32, (rows, lb), 1)
    cnt_c = _branch_count(lb + t_c - i_c)
    t_n = lax.broadcasted_iota(jnp.int32, (rows, rows), 0)
    i_n = lax.broadcasted_iota(jnp.int32, (rows, rows), 1)
    cnt_n = jnp.where(i_n < n_new, _branch_count(t_n - i_n), 0.0)
    low = lax.broadcasted_iota(jnp.int32, (rows, PAIR), 1) < HEAD_DIM
    scale = HEAD_DIM ** -0.5
    for pr in range(N_PAIRS):
        sl = slice(pr * PAIR, (pr + 1) * PAIR)
        qp = q_ref[pr]
        ktp = kt_ref[sl, :].astype(BF16)
        vtp = vt_ref[sl, :].astype(BF16)
        knp = k_new[:, sl].astype(BF16)
        vnp = v_new[:, sl].astype(BF16)
        o_h = []
        for hh in range(2):
            qm = jnp.where(low if hh == 0 else jnp.logical_not(low), qp, 0.0).astype(BF16)
            s_c = jnp.where(cnt_c > 0, _dot(qm, ktp) * scale, -jnp.inf)
            s_n = jnp.where(cnt_n > 0, _dot_nt(qm, knp) * scale, -jnp.inf)
            m = jnp.maximum(jnp.max(s_c, axis=-1, keepdims=True), jnp.max(s_n, axis=-1, keepdims=True))
            p_c = cnt_c * jnp.exp(s_c - m)
            p_n = cnt_n * jnp.exp(s_n - m)
            den = jnp.sum(p_c, axis=-1, keepdims=True) + jnp.sum(p_n, axis=-1, keepdims=True)
            pv = _dot_nt(p_c.astype(BF16), vtp) + _dot(p_n.astype(BF16), vnp)
            o_h.append(pv / den)
        att_ref[pr] = jnp.where(low, o_h[0], o_h[1])


def _attn_sample(q, k_new, v_new, kt_cache, vt_cache, rows, n_new):
    b, w, lb = kt_cache.shape
    assert n_new <= 8 and lb % LANES == 0
    slab = pl.BlockSpec((None, N_PAIRS, rows, PAIR), lambda bi: (0, 0, bi, 0))
    new = pl.BlockSpec((None, rows, w), lambda bi: (0, bi, 0))
    cache = pl.BlockSpec((None, w, lb), lambda bi: (bi, 0, 0))
    return pl.pallas_call(
        functools.partial(_attn_sample_body, n_new=n_new),
        grid=(b,),
        in_specs=[slab, new, new, cache, cache],
        out_specs=(slab, cache, cache),
        out_shape=(jax.ShapeDtypeStruct(q.shape, F32),
                   jax.ShapeDtypeStruct((b, w, lb), F32),
                   jax.ShapeDtypeStruct((b, w, lb), F32)),
        compiler_params=_params("arbitrary"),
        name="attn_sample",
    )(q, k_new, v_new, kt_cache, vt_cache)


def _ssd_body(xbc_ref, z_ref, dt_ref, cprev_ref, h0_ref, cw_ref, cb_ref, dtb_ref, alog_ref, dskip_ref, g_ref,
              y_ref, hout_ref, h_ref, xprev_ref, *, q, chunks, n_valid):
    @pl.when(pl.program_id(1) == 0)
    def _():
        h_ref[...] = h0_ref[0]
        xprev_ref[...] = cprev_ref[0]

    def one_chunk(ck, carry):
        _ssd_chunk(pl.program_id(1) * chunks + ck, pl.multiple_of(ck * q, q),
                   xbc_ref, z_ref, dt_ref, cw_ref, cb_ref, dtb_ref, alog_ref, dskip_ref, g_ref, y_ref,
                   h_ref, xprev_ref, q=q, n_valid=n_valid)
        return carry

    lax.fori_loop(0, chunks, one_chunk, 0)
    hout_ref[0] = h_ref[...]


def _silu(x):
    half = 0.5 * x
    return half + half * jnp.tanh(half)


def _ssd_chunk(chunk, r0, xbc_ref, z_ref, dt_ref, cw_ref, cb_ref, dtb_ref, alog_ref, dskip_ref, g_ref, y_ref,
               h_ref, xprev_ref, *, q, n_valid):
    def chunk_rows(ref):
        return ref[0, pl.ds(r0, q), :]

    def to_columns(t8):
        return jnp.concatenate([t8, jnp.zeros((LANES - 8, q), F32)], axis=0).T

    cur = chunk_rows(xbc_ref)
    prev8 = xprev_ref[...]
    i8 = lax.broadcasted_iota(jnp.int32, (8, CONV_CH), 0)
    acc = cb_ref[...] + cur * cw_ref[CONV_WIDTH - 1:CONV_WIDTH, :]
    for sh in range(1, CONV_WIDTH):
        rolled = pltpu.roll(cur, sh, axis=0)
        head = jnp.where(i8 < sh, pltpu.roll(prev8, sh, axis=0), rolled[0:8])
        shifted = jnp.concatenate([head, rolled[8:]], axis=0)
        acc = acc + shifted * cw_ref[CONV_WIDTH - 1 - sh:CONV_WIDTH - sh, :]
    xprev_ref[...] = cur[q - 8:q]
    conv = _silu(acc)
    xs = conv[:, 0:SSM_WIDTH]

    tok = lax.broadcasted_iota(jnp.int32, (N_SSM_HEADS, q), 1)
    dt_in = chunk_rows(dt_ref).T[DT_LANE0:LANES, :] + dtb_ref[:, 0:q]
    dtv_t = jnp.maximum(dt_in, 0.0) + jnp.log1p(jnp.exp(-jnp.abs(dt_in)))
    dtv_t = jnp.where(chunk * q + tok < n_valid, dtv_t, 0.0)
    a_t = dtv_t * (-jnp.exp(alog_ref[:, 0:q]))
    row_l = lax.broadcasted_iota(jnp.int32, (q, q), 0)
    lane_l = lax.broadcasted_iota(jnp.int32, (q, q), 1)
    tri = row_l >= lane_l
    a_cs_t = jnp.dot(a_t, (row_l <= lane_l).astype(F32), preferred_element_type=F32,
                     precision=lax.Precision.HIGHEST)
    a_cs = to_columns(a_cs_t)
    dtv = to_columns(dtv_t)
    low = lax.broadcasted_iota(jnp.int32, (q, PAIR), 1) < HEAD_DIM

    def per_pair(t, pr):
        return jnp.where(low, t[:, 2 * pr:2 * pr + 1], t[:, 2 * pr + 1:2 * pr + 2])

    n_pairs = N_SSM_HEADS // 2
    pairs_per_group = n_pairs // SSM_GROUPS
    decay_in = jnp.exp(a_cs)
    decay_out = jnp.exp(a_cs[q - 1:q, :] - a_cs)
    total = jnp.exp(a_cs_t[:, q - 1:q])
    bmb, cmb, cb = [], [], []
    for grp in range(SSM_GROUPS):
        b0 = SSM_WIDTH + grp * SSM_STATE
        c0 = SSM_WIDTH + (SSM_GROUPS + grp) * SSM_STATE
        bmb.append(conv[:, b0:b0 + SSM_STATE].astype(BF16))
        cmb.append(conv[:, c0:c0 + SSM_STATE].astype(BF16))
        cb.append(_dot_nt(cmb[grp], bmb[grp]))
    xs_p = [xs[:, pr * PAIR:(pr + 1) * PAIR] for pr in range(n_pairs)]
    xdt = [xs_p[pr] * per_pair(dtv, pr) for pr in range(n_pairs)]
    xdt_b = [t.astype(BF16) for t in xdt]
    scores = [(cb[hd // (N_SSM_HEADS // SSM_GROUPS)]
               * jnp.exp(jnp.where(tri, a_cs[:, hd:hd + 1] - a_cs_t[hd:hd + 1, :], -jnp.inf))).astype(BF16)
              for hd in range(N_SSM_HEADS)]
    y_diag = [jnp.where(low, _dot(scores[2 * pr], xdt_b[pr]), _dot(scores[2 * pr + 1], xdt_b[pr]))
              for pr in range(n_pairs)]
    states = [_dot_tn((xdt[pr] * per_pair(decay_out, pr)).astype(BF16), bmb[pr // pairs_per_group])
              for pr in range(n_pairs)]
    h_old = [h_ref[pr * PAIR:(pr + 1) * PAIR, :] for pr in range(n_pairs)]
    y_off = [_dot_nt(cmb[pr // pairs_per_group], h_old[pr].astype(BF16)) * per_pair(decay_in, pr)
             for pr in range(n_pairs)]
    ys = []
    for pr in range(n_pairs):
        chunk_decay = jnp.concatenate([jnp.broadcast_to(total[2 * pr:2 * pr + 1], (HEAD_DIM, SSM_STATE)),
                                       jnp.broadcast_to(total[2 * pr + 1:2 * pr + 2], (HEAD_DIM, SSM_STATE))],
                                      axis=0)
        h_ref[pr * PAIR:(pr + 1) * PAIR, :] = h_old[pr] * chunk_decay + states[pr]
        ys.append(y_diag[pr] + y_off[pr] + dskip_ref[:, pr * PAIR:(pr + 1) * PAIR] * xs_p[pr])
    y = jnp.concatenate(ys, axis=1)

    u = y * _silu(chunk_rows(z_ref))
    gw = SSM_WIDTH // SSM_GROUPS
    normed = []
    for grp in range(SSM_GROUPS):
        ug = u[:, grp * gw:(grp + 1) * gw]
        normed.append(ug * lax.rsqrt(jnp.mean(ug * ug, axis=-1, keepdims=True) + EPS))
    out = jnp.concatenate(normed, axis=1) * g_ref[...]
    y_ref[0, pl.ds(r0, q), :] = out


def _ssd(xbc, z, dt, conv_prev8, h0, conv_w8, conv_b, dt_bias, a_log, d_skip, g_ssm, n_valid):
    b, l, _ = xbc.shape
    q = min(SSD_CHUNK, l)
    chunks = 4 if l % (4 * q) == 0 else 1
    rows_in = q * chunks
    nc = l // rows_in
    row = lambda bi, c: (bi, c, 0)
    per_b = lambda bi, c: (bi, 0, 0)
    return pl.pallas_call(
        functools.partial(_ssd_body, q=q, chunks=chunks, n_valid=n_valid),
        grid=(b, nc),
        in_specs=[
            pl.BlockSpec((1, rows_in, CONV_CH), row),
            pl.BlockSpec((1, rows_in, SSM_WIDTH), row),
            pl.BlockSpec((1, rows_in, LANES), row),
            pl.BlockSpec((1, 8, CONV_CH), per_b),
            pl.BlockSpec((1, SSM_WIDTH, SSM_STATE), per_b),
            _const_spec((8, CONV_CH)),
            _const_spec((1, CONV_CH)),
            _const_spec((N_SSM_HEADS, SSD_CHUNK)),
            _const_spec((N_SSM_HEADS, SSD_CHUNK)),
            _const_spec((1, SSM_WIDTH)),
            _const_spec((1, SSM_WIDTH)),
        ],
        out_specs=(pl.BlockSpec((1, rows_in, SSM_WIDTH), row),
                   pl.BlockSpec((1, SSM_WIDTH, SSM_STATE), per_b)),
        out_shape=(jax.ShapeDtypeStruct((b, l, SSM_WIDTH), F32),
                   jax.ShapeDtypeStruct((b, SSM_WIDTH, SSM_STATE), F32)),
        scratch_shapes=[pltpu.VMEM((SSM_WIDTH, SSM_STATE), F32), pltpu.VMEM((8, CONV_CH), F32)],
        compiler_params=_params("arbitrary", "arbitrary"),
        name="ssd",
    )(xbc, z, dt, conv_prev8, h0, conv_w8, conv_b, dt_bias, a_log, d_skip, g_ssm)


def _post1_body(x_ref, att_ref, y_ref, wout_ref, g_ref, wxq_ref, x1_ref, qx_ref):
    mixed = jnp.concatenate([att_ref[0, pr] for pr in range(N_PAIRS)] + [y_ref[0]], axis=1).astype(BF16)
    x1 = x_ref[0] + _dot(mixed, wout_ref[...].astype(BF16))
    x1_ref[0] = x1
    qx_ref[0] = _dot(_rmsnorm(x1, g_ref[...]).astype(BF16), wxq_ref[...].astype(BF16)).astype(BF16)


def _post1(x, att, y, w_out, g_xatt, w_xq, tm):
    b, l, _ = x.shape
    row = lambda bi, j: (bi, j, 0)
    full = pl.BlockSpec((1, tm, D_MODEL), row)
    return pl.pallas_call(
        _post1_body,
        grid=(b, l // tm),
        in_specs=[full,
                  pl.BlockSpec((1, N_PAIRS, tm, PAIR), lambda bi, j: (bi, 0, j, 0)),
                  pl.BlockSpec((1, tm, SSM_WIDTH), row),
                  _const_spec(w_out.shape), _const_spec((1, D_MODEL)), _const_spec(w_xq.shape)],
        out_specs=(full, full),
        out_shape=(jax.ShapeDtypeStruct((b, l, D_MODEL), F32), jax.ShapeDtypeStruct((b, l, D_MODEL), BF16)),
        compiler_params=_params("arbitrary", "arbitrary"),
        name="post1",
    )(x, att, y, w_out, g_xatt, w_xq)


def _mem_kv_body(m_ref, g_ref, wk_ref, wv_ref, k_ref, v_ref, kb_ref, vb_ref):
    h = _rmsnorm(m_ref[...], g_ref[...]).astype(BF16)
    k = _dot(h, wk_ref[...].astype(BF16))
    v = _dot(h, wv_ref[...].astype(BF16))
    k_ref[...] = k
    v_ref[...] = v
    kb_ref[...] = k.astype(BF16)
    vb_ref[...] = v.astype(BF16)


def _mem_kv(mem, g_mem, w_mk, w_mv, tm):
    n, _ = mem.shape
    row = pl.BlockSpec((tm, D_MODEL), lambda i: (i, 0))
    return pl.pallas_call(
        _mem_kv_body,
        grid=(n // tm,),
        in_specs=[row, _const_spec((1, D_MODEL)), _const_spec(w_mk.shape), _const_spec(w_mv.shape)],
        out_specs=(row, row, row, row),
        out_shape=(jax.ShapeDtypeStruct((n, D_MODEL), F32), jax.ShapeDtypeStruct((n, D_MODEL), F32),
                   jax.ShapeDtypeStruct((n, D_MODEL), BF16), jax.ShapeDtypeStruct((n, D_MODEL), BF16)),
        compiler_params=_params("arbitrary"),
        name="mem_kv",
    )(mem, g_mem, w_mk, w_mv)


def _xattn_body(q_ref, mk_ref, mv_ref, o_ref, *, tiled):
    scale = XATT_HEAD_DIM ** -0.5
    chunks = XATT_HEAD_DIM // LANES

    def head(ref, gi, hd):
        if not tiled:
            return ref[gi, :, hd * XATT_HEAD_DIM:(hd + 1) * XATT_HEAD_DIM]
        return jnp.concatenate(
            [ref[gi, pl.ds(c * N_XATT_HEADS + hd, N_MEM, stride=chunks * N_XATT_HEADS), :] for c in range(chunks)],
            axis=1).astype(BF16)

    for gi in range(q_ref.shape[0]):
        for hd in range(N_XATT_HEADS):
            sl = slice(hd * XATT_HEAD_DIM, (hd + 1) * XATT_HEAD_DIM)
            s = _dot_nt(q_ref[gi, :, sl], head(mk_ref, gi, hd)) * scale
            p = jnp.exp(s - jnp.max(s, axis=-1, keepdims=True))
            den = jnp.sum(p, axis=-1, keepdims=True)
            o = _dot(p.astype(BF16), head(mv_ref, gi, hd)) / den
            o_ref[gi, :, sl] = o.astype(BF16)


def _xattn(qx, mem_k, mem_v, gb, tm):
    b, l, _ = qx.shape
    tiled = mem_k.shape[-1] == LANES
    qspec = pl.BlockSpec((gb, tm, D_MODEL), lambda bi, j: (bi, j, 0))
    mspec = pl.BlockSpec((gb,) + mem_k.shape[1:], lambda bi, j: (bi, 0, 0))
    return pl.pallas_call(
        functools.partial(_xattn_body, tiled=tiled),
        grid=(b // gb, l // tm),
        in_specs=[qspec, mspec, mspec],
        out_specs=qspec,
        out_shape=jax.ShapeDtypeStruct((b, l, D_MODEL), BF16),
        compiler_params=_params("arbitrary", "arbitrary"),
        name="xattn",
    )(qx, mem_k, mem_v)


def _post2_body(x1_ref, o_ref, wxo_ref, gmlp_ref, wup_ref, wdown_ref, gfin_ref, y_ref):
    x2 = x1_ref[...] + _dot(o_ref[...], wxo_ref[...].astype(BF16))
    hm = _rmsnorm(x2, gmlp_ref[...]).astype(BF16)
    acc = x2
    for c in range(D_FF // D_MODEL):
        sl = slice(c * D_MODEL, (c + 1) * D_MODEL)
        u = jnp.maximum(_dot(hm, wup_ref[:, sl].astype(BF16)), 0.0)
        acc = acc + _dot((u * u).astype(BF16), wdown_ref[sl, :].astype(BF16))
    y_ref[...] = _rmsnorm(acc, gfin_ref[...])


def _post2(x1, o, w_xo, g_mlp, w_up, w_down, g_final, tm):
    n, _ = x1.shape
    row = pl.BlockSpec((tm, D_MODEL), lambda i: (i, 0))
    return pl.pallas_call(
        _post2_body,
        grid=(n // tm,),
        in_specs=[row, row, _const_spec(w_xo.shape), _const_spec((1, D_MODEL)), _const_spec(w_up.shape),
                  _const_spec(w_down.shape), _const_spec((1, D_MODEL))],
        out_specs=row,
        out_shape=jax.ShapeDtypeStruct((n, D_MODEL), F32),
        compiler_params=_params("arbitrary"),
        name="post2",
    )(x1, o, w_xo, g_mlp, w_up, w_down, g_final)


def _rope_tables(pos):
    half = HEAD_DIM // 2
    inv = ROPE_THETA ** (-np.arange(half, dtype=np.float64) * 2.0 / HEAD_DIM)
    ang = np.asarray(pos, np.float64)[:, None] * inv[None, :]
    c, s = np.cos(ang), np.sin(ang)
    return (jnp.asarray(np.concatenate([c, c, c, c], axis=1), F32),
            jnp.asarray(np.concatenate([-s, s, -s, s], axis=1), F32))


def kernel(x_prompt, x_sample, cache_win_k, cache_win_v, state_conv, state_ssm, cache_mem_k, cache_mem_v,
           mem_prompt, g_mix, w_in, conv_w, conv_b, dt_bias, a_log, d_skip, g_ssm, w_out, g_xatt, g_mem,
           w_xq, w_mk, w_mv, w_xo, g_mlp, w_up, w_down, g_final):
    depth = w_in.shape[0]
    assert depth == 1, "kernel is written for the single-layer trunk of this problem"
    bp, s_len, _ = x_prompt.shape
    bs, t_new, _ = x_sample.shape
    lb = cache_win_k.shape[2]
    lw = min(MAX_WINDOW, s_len)

    li = 0
    row = lambda t: t[li].reshape(1, -1)
    w_t = jnp.transpose(w_in[li])
    conv_w8 = jnp.pad(conv_w[li], ((0, 8 - CONV_WIDTH), (0, 0)))
    per_head_rows = lambda t: jnp.broadcast_to(t[li][:, None], (N_SSM_HEADS, SSD_CHUNK))
    dt_b = per_head_rows(dt_bias)
    a_lg = per_head_rows(a_log)
    d_sk = jnp.repeat(d_skip[li], HEAD_DIM).reshape(1, -1)
    w_o, w_q, w_k, w_v, w_x, w_u, w_d = (t[li] for t in (w_out, w_xq, w_mk, w_mv, w_xo, w_up, w_down))
    g_fin = g_final.reshape(1, -1)

    cos_p, sin_p = _rope_tables(np.arange(s_len))
    q, k, v, k32, v32, z, xbc, dt = _in_proj(x_prompt, row(g_mix), w_t, cos_p, sin_p, lw, ROW_TILE)
    att = _attn_prompt(q, k, v)
    y_ssm, ssm_p = _ssd(xbc, z, dt, jnp.zeros((bp, 8, CONV_CH), F32),
                        jnp.zeros((bp, SSM_WIDTH, SSM_STATE), F32), conv_w8, row(conv_b), dt_b, a_lg, d_sk,
                        row(g_ssm), s_len)
    mk, mv, mk_b, mv_b = _mem_kv(mem_prompt.reshape(bp * N_MEM, D_MODEL), row(g_mem), w_k, w_v, ROW_TILE)
    x1, qx = _post1(x_prompt, att, y_ssm, w_o, row(g_xatt), w_q, ROW_TILE)
    o = _xattn(qx, mk_b.reshape(bp, N_MEM, D_MODEL), mv_b.reshape(bp, N_MEM, D_MODEL), 1, ROW_TILE)
    y_prompt = _post2(x1.reshape(bp * s_len, D_MODEL), o.reshape(bp * s_len, D_MODEL), w_x, row(g_mlp),
                      w_u, w_d, g_fin, ROW_TILE)
    y_prompt = y_prompt.reshape(bp, s_len, D_MODEL)
    win_k_p = k32.reshape(1, bp, lw, N_ATT_HEADS, HEAD_DIM)
    win_v_p = v32.reshape(1, bp, lw, N_ATT_HEADS, HEAD_DIM)
    conv_p = xbc[:, s_len - (CONV_WIDTH - 1):][None]
    ssm_p = ssm_p.reshape(1, bp, N_SSM_HEADS, HEAD_DIM, SSM_STATE)
    mk_p = mk.reshape(1, bp, N_MEM, N_XATT_HEADS, XATT_HEAD_DIM)
    mv_p = mv.reshape(1, bp, N_MEM, N_XATT_HEADS, XATT_HEAD_DIM)

    r = SAMPLE_ROWS
    n_s = bs * r
    xs_pad = jnp.pad(x_sample, ((0, 0), (0, r - t_new), (0, 0))).reshape(1, n_s, D_MODEL)
    pos_s = PAST_LEN + (np.arange(n_s) % r)
    cos_s, sin_s = _rope_tables(pos_s)
    q, _, _, k32, v32, z, xbc, dt = _in_proj(xs_pad, row(g_mix), w_t, cos_s, sin_s, n_s, ROW_TILE)
    to_minor = lambda c: jnp.transpose(c[li].reshape(bs, lb, ATT_WIDTH), (0, 2, 1))
    att, kt_s, vt_s = _attn_sample(q, k32, v32, to_minor(cache_win_k), to_minor(cache_win_v), r, t_new)
    per_seq = lambda t: t.reshape(bs, r, t.shape[-1])
    conv_prev8 = jnp.pad(state_conv[li], ((0, 0), (8 - (CONV_WIDTH - 1), 0), (0, 0)))
    xbc_s = per_seq(xbc)
    y_ssm, ssm_s = _ssd(xbc_s, per_seq(z), per_seq(dt), conv_prev8,
                        state_ssm[li].reshape(bs, SSM_WIDTH, SSM_STATE), conv_w8, row(conv_b), dt_b, a_lg, d_sk,
                        row(g_ssm), t_new)
    x1, qx = _post1(xs_pad, att, y_ssm.reshape(1, n_s, SSM_WIDTH), w_o, row(g_xatt), w_q, ROW_TILE)
    mem_rows = lambda c: c[li].reshape(bs, N_MEM, N_XATT_HEADS, XATT_HEAD_DIM // LANES, LANES).transpose(
        0, 1, 3, 2, 4).reshape(bs, N_MEM * N_XATT_HEADS * (XATT_HEAD_DIM // LANES), LANES)
    o = _xattn(qx.reshape(bs, r, D_MODEL), mem_rows(cache_mem_k), mem_rows(cache_mem_v), 4, r)
    y_s = _post2(x1.reshape(n_s, D_MODEL), o.reshape(n_s, D_MODEL), w_x, row(g_mlp), w_u, w_d, g_fin, ROW_TILE)
    y_sample = y_s.reshape(bs, r, D_MODEL)[:, :t_new]
    from_minor = lambda c: jnp.transpose(c, (0, 2, 1)).reshape(1, bs, lb, N_ATT_HEADS, HEAD_DIM)
    win_k_s, win_v_s = from_minor(kt_s), from_minor(vt_s)
    conv_s = jnp.concatenate([state_conv[li], xbc_s[:, :t_new]], axis=1)[:, -(CONV_WIDTH - 1):][None]
    ssm_s = ssm_s.reshape(1, bs, N_SSM_HEADS, HEAD_DIM, SSM_STATE)

    return (y_prompt, y_sample, win_k_p, win_v_p, conv_p, ssm_p, mk_p, mv_p,
            win_k_s, win_v_s, conv_s, ssm_s)
```
